```python
import math
import jax, jax.numpy as jnp
from jax import lax
import numpy as np

D_MODEL = 2048
BATCH = 4
SEQ = 2048
DEPTH = 4

N_A_LAYERS = DEPTH // 2
N_B_LAYERS = DEPTH - N_A_LAYERS
CHUNK = 128
A_GROUPS = 8
A_HALF = D_MODEL
HEAD_DIM = 64
N_HEADS = D_MODEL // HEAD_DIM
N_KV_HEADS = N_HEADS // 8
WINDOW = 128
BLOCK = WINDOW
N_BUCKETS = 32
MAX_DISTANCE = 128
D_FF = ((8 * D_MODEL // 3 + 255) // 256) * 256
RMS_EPS = 1e-5
NEG_INF = -1e30

kernel_name = "yoco_gmlp_swa_sink_hybrid"


def rmsnorm(x, g):
    x32 = x.astype(jnp.float32)
    y = x32 * lax.rsqrt(jnp.mean(x32 * x32, axis=-1, keepdims=True) + RMS_EPS)
    return (y * g.astype(jnp.float32)).astype(x.dtype)


def swiglu(x, w_gate, w_up, w_down):
    return (jax.nn.silu(x @ w_gate) * (x @ w_up)) @ w_down


def gmlp_mixer(xn, w_in, norm_v, w_s, b_s, w_out):
    B, S, _ = xn.shape
    n_chunks = S // CHUNK
    z = jax.nn.gelu(xn @ w_in)
    u, v = jnp.split(z, 2, axis=-1)
    v = rmsnorm(v, norm_v)
    v = v.reshape(B, n_chunks, CHUNK, A_GROUPS, A_HALF // A_GROUPS)
    w_causal = jnp.tril(w_s)
    s = jnp.einsum('gts,bcsgd->bctgd', w_causal, v) + b_s.T[None, None, :, :, None]
    gated = u * s.reshape(B, S, A_HALF)
    return gated @ w_out


def t5_bucket(dist):
    max_exact = N_BUCKETS // 2
    is_small = dist < max_exact
    d = jnp.maximum(dist, 1).astype(jnp.float32)
    large = max_exact + (jnp.log(d / max_exact) / math.log(MAX_DISTANCE / max_exact)
                         * (N_BUCKETS - max_exact)).astype(jnp.int32)
    large = jnp.minimum(large, N_BUCKETS - 1)
    return jnp.where(is_small, dist, large)


def banded_sink_attention(q, k, v, sinks, rel_bias):
    B, S = q.shape[0], q.shape[1]
    nb = S // BLOCK
    grp = N_HEADS // N_KV_HEADS
    qb = q.reshape(B, nb, BLOCK, N_KV_HEADS, grp, HEAD_DIM).astype(jnp.float32)

    def band(t):
        tp = jnp.pad(t, ((0, 0), (BLOCK, 0), (0, 0), (0, 0)))
        prev = tp[:, :S].reshape(B, nb, BLOCK, N_KV_HEADS, HEAD_DIM)
        cur = t.reshape(B, nb, BLOCK, N_KV_HEADS, HEAD_DIM)
        return jnp.concatenate([prev, cur], axis=2)

    kb = band(k).astype(jnp.float32)
    vb = band(v)
    scores = jnp.einsum('bcqhgd,bckhd->bchgqk', qb, kb) / math.sqrt(HEAD_DIM)

    dist = np.arange(BLOCK)[:, None] + BLOCK - np.arange(2 * BLOCK)[None, :]
    in_window = (dist >= 0) & (dist < WINDOW)
    bucket = t5_bucket(jnp.asarray(np.clip(dist, 0, None), dtype=jnp.int32))
    bias = rel_bias[bucket].astype(jnp.float32)
    bias = bias.transpose(2, 0, 1).reshape(N_KV_HEADS, grp, BLOCK, 2 * BLOCK)
    key_exists = (np.arange(nb)[:, None] * BLOCK - BLOCK + np.arange(2 * BLOCK)[None, :]) >= 0
    mask = in_window[None] & key_exists[:, None, :]
    mask = jnp.asarray(mask)[None, :, None, None]
    scores = jnp.where(mask, scores + bias, NEG_INF)

    sink = sinks.astype(jnp.float32).reshape(N_KV_HEADS, grp)[None, None, :, :, None, None]
    m = jnp.maximum(scores.max(axis=-1, keepdims=True), sink)
    p = jnp.exp(scores - m)
    denom = p.sum(axis=-1, keepdims=True) + jnp.exp(sink - m)
    probs = (p / denom).astype(v.dtype)
    out = jnp.einsum('bchgqk,bckhd->bcqhgd', probs, vb)
    return out.reshape(B, S, N_HEADS * HEAD_DIM)


def setup_inputs(seed: int = 0) -> dict:
    key = jax.random.key(seed)
    ks = jax.random.split(key, 24)
    f32 = jnp.float32
    out_scale = (2.0 * DEPTH) ** -0.5

    def nrm(k, shape, scale):
        return jax.random.normal(k, shape, f32) * scale

    kv_dim = 2 * N_KV_HEADS * HEAD_DIM
    return {
        "x": nrm(ks[0], (BATCH, SEQ, D_MODEL), 1.0),
        "mix_norm": 1.0 + nrm(ks[1], (DEPTH, D_MODEL), 0.1),
        "ffn_norm": 1.0 + nrm(ks[2], (DEPTH, D_MODEL), 0.1),
        "a_w_in": nrm(ks[3], (N_A_LAYERS, D_MODEL, 2 * A_HALF), D_MODEL ** -0.5),
        "a_norm_v": 1.0 + nrm(ks[4], (N_A_LAYERS, A_HALF), 0.1),
        "a_w_s": nrm(ks[5], (N_A_LAYERS, A_GROUPS, CHUNK, CHUNK), CHUNK ** -0.5),
        "a_b_s": 1.0 + nrm(ks[6], (N_A_LAYERS, A_GROUPS, CHUNK), 0.1),
        "a_w_out": nrm(ks[7], (N_A_LAYERS, A_HALF, D_MODEL), A_HALF ** -0.5 * out_scale),
        "kv_norm": 1.0 + nrm(ks[8], (D_MODEL,), 0.1),
        "w_kv": nrm(ks[9], (D_MODEL, kv_dim), D_MODEL ** -0.5),
        "b_kv": nrm(ks[10], (kv_dim,), 0.02),
        "b_w_q": nrm(ks[11], (N_B_LAYERS, D_MODEL, N_HEADS * HEAD_DIM), D_MODEL ** -0.5),
        "b_b_q": nrm(ks[12], (N_B_LAYERS, N_HEADS * HEAD_DIM), 0.02),
        "b_sinks": nrm(ks[13], (N_B_LAYERS, N_HEADS), 1.0),
        "b_w_o": nrm(ks[14], (N_B_LAYERS, N_HEADS * HEAD_DIM, D_MODEL), (N_HEADS * HEAD_DIM) ** -0.5 * out_scale),
        "b_b_o": nrm(ks[15], (N_B_LAYERS, D_MODEL), 0.02),
        "rel_bias": nrm(ks[16], (N_BUCKETS, N_HEADS), 0.5),
        "ffn_w_gate": nrm(ks[17], (DEPTH, D_MODEL, D_FF), D_MODEL ** -0.5),
        "ffn_w_up": nrm(ks[18], (DEPTH, D_MODEL, D_FF), D_MODEL ** -0.5),
        "ffn_w_down": nrm(ks[19], (DEPTH, D_FF, D_MODEL), D_FF ** -0.5 * out_scale),
        "final_norm": 1.0 + nrm(ks[20], (D_MODEL,), 0.1),
    }


def reference(x, mix_norm, ffn_norm, a_w_in, a_norm_v, a_w_s, a_b_s, a_w_out,
              kv_norm, w_kv, b_kv, b_w_q, b_b_q, b_sinks, b_w_o, b_b_o, rel_bias,
              ffn_w_gate, ffn_w_up, ffn_w_down, final_norm):
    B, S, _ = x.shape
    h = x
    k_shared = None
    v_shared = None
    for layer in range(DEPTH):
        xn = rmsnorm(h, mix_norm[layer])
        if layer < N_A_LAYERS:
            i = layer
            h = h + gmlp_mixer(xn, a_w_in[i], a_norm_v[i], a_w_s[i], a_b_s[i], a_w_out[i])
        else:
            i = layer - N_A_LAYERS
            q = (xn @ b_w_q[i] + b_b_q[i]).reshape(B, S, N_HEADS, HEAD_DIM)
            attn = banded_sink_attention(q, k_shared, v_shared, b_sinks[i], rel_bias)
            h = h + attn @ b_w_o[i] + b_b_o[i]
        h = h + swiglu(rmsnorm(h, ffn_norm[layer]), ffn_w_gate[layer], ffn_w_up[layer], ffn_w_down[layer])
        if layer == N_A_LAYERS - 1:
            kv = rmsnorm(h, kv_norm) @ w_kv + b_kv
            k_flat, v_flat = jnp.split(kv, 2, axis=-1)
            k_shared = k_flat.reshape(B, S, N_KV_HEADS, HEAD_DIM)
            v_shared = v_flat.reshape(B, S, N_KV_HEADS, HEAD_DIM)
    return rmsnorm(h, final_norm)
```

```python
import functools
import math

import jax
import jax.numpy as jnp
import numpy as np
from jax import lax
from jax.experimental import pallas as pl
from jax.experimental.pallas import tpu as pltpu

F32 = jnp.float32
BF16 = jnp.bfloat16

RMS_EPS = 1e-5
NEG_INF = -1e30

CHUNK = 128
A_GROUPS = 8
HEAD_DIM = 64
KV_GROUP = 8
WINDOW = 128
N_BUCKETS = 32
MAX_DISTANCE = 128

V7X_VMEM_BYTES = 64 * 1024 * 1024
VMEM_LIMIT_BYTES = V7X_VMEM_BYTES - 8 * 1024 * 1024

TM_LINEAR = 1024
TN_LINEAR = 1024
TM_GMLP = 1024
TN_GMLP = 512
TM_FFN = 512
TF_FFN = 512
NORM_ROWS = 256


def _params():
    return pltpu.CompilerParams(
        dimension_semantics=("arbitrary", "arbitrary"),
        vmem_limit_bytes=VMEM_LIMIT_BYTES,
    )


def _rmsnorm_rows(x_ref, g_ref, o_ref):
    rows = x_ref.shape[0]
    g = g_ref[...]

    def body(i, carry):
        r = pl.ds(pl.multiple_of(i * NORM_ROWS, NORM_ROWS), NORM_ROWS)
        x = x_ref[r, :]
        ms = jnp.mean(x * x, axis=-1, keepdims=True)
        o_ref[r, :] = ((x * lax.rsqrt(ms + RMS_EPS)) * g).astype(o_ref.dtype)
        return carry

    lax.fori_loop(0, rows // NORM_ROWS, body, 0)


def _linear_kernel(*refs, norm, bias, res, scale):
    refs = list(refs)
    x_ref = refs.pop(0)
    g_ref = refs.pop(0) if norm else None
    w_ref = refs.pop(0)
    b_ref = refs.pop(0) if bias else None
    r_ref = refs.pop(0) if res else None
    o_ref = refs.pop(0)
    xn_ref = refs.pop(0) if norm else None

    if norm:
        @pl.when(pl.program_id(1) == 0)
        def _():
            _rmsnorm_rows(x_ref, g_ref, xn_ref)
        lhs = xn_ref[...]
    else:
        lhs = x_ref[...]
    acc = jnp.dot(lhs, w_ref[...], preferred_element_type=F32)
    if bias:
        acc = acc + b_ref[...]
    if scale != 1.0:
        acc = acc * scale
    if res:
        acc = acc + r_ref[...]
    o_ref[...] = acc.astype(o_ref.dtype)


def _linear(x, w, *, gain=None, bias=None, residual=None, scale=1.0, out_dtype, name):
    t, k = x.shape
    n = w.shape[1]
    tm = min(TM_LINEAR, t)
    tn = min(TN_LINEAR, n)
    norm = gain is not None
    in_specs = [pl.BlockSpec((tm, k), lambda m, j: (m, 0))]
    args = [x]
    if norm:
        in_specs.append(pl.BlockSpec((1, k), lambda m, j: (0, 0)))
        args.append(gain.reshape(1, k))
    in_specs.append(pl.BlockSpec((k, tn), lambda m, j: (0, j)))
    args.append(w)
    if bias is not None:
        in_specs.append(pl.BlockSpec((1, tn), lambda m, j: (0, j)))
        args.append(bias.reshape(1, n))
    if residual is not None:
        in_specs.append(pl.BlockSpec((tm, tn), lambda m, j: (m, j)))
        args.append(residual)
    return pl.pallas_call(
        functools.partial(_linear_kernel, norm=norm, bias=bias is not None,
                          res=residual is not None, scale=scale),
        grid=(t // tm, n // tn),
        in_specs=in_specs,
        out_specs=pl.BlockSpec((tm, tn), lambda m, j: (m, j)),
        out_shape=jax.ShapeDtypeStruct((t, n), out_dtype),
        scratch_shapes=[pltpu.VMEM((tm, k), BF16)] if norm else [],
        compiler_params=_params(),
        name=name,
    )(*args)


def _gmlp_in_kernel(h_ref, g_ref, w_ref, nv_ref, ws_ref, bs_ref, o_ref,
                    xn_ref, vbuf_ref, vn_ref, wt_ref, *, nj):
    j = pl.program_id(1)
    tm = h_ref.shape[0]
    tn = w_ref.shape[1]
    gw = vn_ref.shape[2]
    gpt = tn // gw

    @pl.when(j == 0)
    def _():
        _rmsnorm_rows(h_ref, g_ref, xn_ref)

    z = jax.nn.gelu(jnp.dot(xn_ref[...], w_ref[...], preferred_element_type=F32))

    @pl.when(j < nj)
    def _():
        vbuf_ref[j] = z

    @pl.when(j == nj)
    def _():
        ssq = jnp.zeros((tm, 1), F32)
        for jj in range(nj):
            v = vbuf_ref[jj]
            ssq = ssq + jnp.sum(v * v, axis=-1, keepdims=True)
        rinv = lax.rsqrt(ssq / (nj * tn) + RMS_EPS)
        for g in range(nj * gpt):
            v = vbuf_ref[g // gpt][:, (g % gpt) * gw:(g % gpt + 1) * gw]
            nv = nv_ref[:, g * gw:(g + 1) * gw]
            vn_ref[g] = ((v * rinv) * nv).astype(BF16)
        row = lax.broadcasted_iota(jnp.int32, (CHUNK, CHUNK), 0)
        col = lax.broadcasted_iota(jnp.int32, (CHUNK, CHUNK), 1)
        for g in range(nj * gpt):
            wt_ref[g] = jnp.where(col <= row, ws_ref[g], 0.0).astype(BF16)

    @pl.when(j >= nj)
    def _():
        ju = j - nj
        for gl in range(gpt):
            g = ju * gpt + gl
            wt = wt_ref[g]
            b = bs_ref[g]
            for c in range(tm // CHUNK):
                rows = slice(c * CHUNK, (c + 1) * CHUNK)
                s = jnp.dot(wt, vn_ref[g, rows, :], preferred_element_type=F32) + b
                u = z[rows, gl * gw:(gl + 1) * gw]
                o_ref[rows, gl * gw:(gl + 1) * gw] = (u * s).astype(o_ref.dtype)


def _gmlp_in(h, gain, w_in, norm_v, w_s, b_s):
    t, d = h.shape
    half = w_in.shape[1] // 2
    groups = w_s.shape[0]
    gw = half // groups
    tm, tn = TM_GMLP, TN_GMLP
    nj = half // tn
    return pl.pallas_call(
        functools.partial(_gmlp_in_kernel, nj=nj),
        grid=(t // tm, 2 * nj),
        in_specs=[
            pl.BlockSpec((tm, d), lambda m, j: (m, 0)),
            pl.BlockSpec((1, d), lambda m, j: (0, 0)),
            pl.BlockSpec((d, tn), lambda m, j: (0, (j + nj) % (2 * nj))),
            pl.BlockSpec((1, half), lambda m, j: (0, 0)),
            pl.BlockSpec((groups, CHUNK, CHUNK), lambda m, j: (0, 0, 0)),
            pl.BlockSpec((groups, CHUNK, 1), lambda m, j: (0, 0, 0)),
        ],
        out_specs=pl.BlockSpec((tm, tn), lambda m, j: (m, jnp.maximum(j - nj, 0))),
        out_shape=jax.ShapeDtypeStruct((t, half), BF16),
        scratch_shapes=[
            pltpu.VMEM((tm, d), BF16),
            pltpu.VMEM((nj, tm, tn), F32),
            pltpu.VMEM((groups, tm, gw), BF16),
            pltpu.VMEM((groups, CHUNK, CHUNK), BF16),
        ],
        compiler_params=_params(),
        name="gmlp_in",
    )(h, gain.reshape(1, d), w_in, norm_v.reshape(1, half), w_s, b_s.reshape(groups, CHUNK, 1))


def _ffn_kernel(*refs, final):
    refs = list(refs)
    h_ref, g_ref, wg_ref, wu_ref, wd_ref = refs[:5]
    fg_ref = refs[5] if final else None
    o_ref, xn_ref = refs[-2:]
    f = pl.program_id(1)
    d = o_ref.shape[1]
    tn = wg_ref.shape[1]

    @pl.when(f == 0)
    def _():
        _rmsnorm_rows(h_ref, g_ref, xn_ref)
        o_ref[...] = h_ref[...]

    xn = xn_ref[...]
    gate = jnp.dot(xn, wg_ref[...], preferred_element_type=F32)
    up = jnp.dot(xn, wu_ref[...], preferred_element_type=F32)
    a = (jax.nn.silu(gate) * up).astype(BF16)
    for n in range(d // tn):
        cols = slice(n * tn, (n + 1) * tn)
        o_ref[:, cols] += jnp.dot(a, wd_ref[:, cols], preferred_element_type=F32)

    if final:
        @pl.when(f == pl.num_programs(1) - 1)
        def _():
            _rmsnorm_rows(o_ref, fg_ref, o_ref)


def _ffn(h, gain, wg, wu, wd, final_gain=None):
    t, d = h.shape
    ff = wg.shape[1]
    tm, tf = TM_FFN, TF_FFN
    final = final_gain is not None
    in_specs = [
        pl.BlockSpec((tm, d), lambda m, f: (m, 0)),
        pl.BlockSpec((1, d), lambda m, f: (0, 0)),
        pl.BlockSpec((d, tf), lambda m, f: (0, f)),
        pl.BlockSpec((d, tf), lambda m, f: (0, f)),
        pl.BlockSpec((tf, d), lambda m, f: (f, 0)),
    ]
    args = [h, gain.reshape(1, d), wg, wu, wd]
    if final:
        in_specs.append(pl.BlockSpec((1, d), lambda m, f: (0, 0)))
        args.append(final_gain.reshape(1, d))
    return pl.pallas_call(
        functools.partial(_ffn_kernel, final=final),
        grid=(t // tm, ff // tf),
        in_specs=in_specs,
        out_specs=pl.BlockSpec((tm, d), lambda m, f: (m, 0)),
        out_shape=jax.ShapeDtypeStruct((t, d), F32),
        scratch_shapes=[pltpu.VMEM((tm, d), BF16)],
        compiler_params=_params(),
        name="ffn",
    )(*args)


def _attn_kernel(sink_ref, q_ref, kvp_ref, kvc_ref, bias_ref, o_ref, *, n_kv):
    c = pl.program_id(1)
    kvd = n_kv * HEAD_DIM
    col = lax.broadcasted_iota(jnp.int32, (WINDOW, 2 * WINDOW), 1)
    no_prev = jnp.logical_and(col < WINDOW, c == 0)
    for hk in range(n_kv):
        ks = slice(hk * HEAD_DIM, (hk + 1) * HEAD_DIM)
        vs = slice(kvd + hk * HEAD_DIM, kvd + (hk + 1) * HEAD_DIM)
        k = jnp.concatenate([kvp_ref[0, :, ks], kvc_ref[0, :, ks]], axis=0)
        v = jnp.concatenate([kvp_ref[0, :, vs], kvc_ref[0, :, vs]], axis=0)
        for g in range(KV_GROUP):
            h = hk * KV_GROUP + g
            hs = slice(h * HEAD_DIM, (h + 1) * HEAD_DIM)
            s = lax.dot_general(q_ref[0, :, hs], k, (((1,), (1,)), ((), ())),
                                preferred_element_type=F32)
            s = jnp.where(no_prev, NEG_INF, s + bias_ref[h])
            sink = sink_ref[h]
            m = jnp.maximum(jnp.max(s, axis=-1, keepdims=True), sink)
            p = jnp.exp(s - m)
            denom = jnp.sum(p, axis=-1, keepdims=True) + jnp.exp(sink - m)
            o = jnp.dot(p.astype(BF16), v, preferred_element_type=F32)
            o_ref[0, :, hs] = (o / denom).astype(o_ref.dtype)


def _attention(q, kv, sinks, bias, *, batch, seq):
    t, qd = q.shape
    kvd2 = kv.shape[1]
    n_heads = qd // HEAD_DIM
    n_kv = kvd2 // (2 * HEAD_DIM)
    nb = seq // WINDOW
    q3 = q.reshape(batch, seq, qd)
    kv3 = kv.reshape(batch, seq, kvd2)
    out = pl.pallas_call(
        functools.partial(_attn_kernel, n_kv=n_kv),
        grid=(batch, nb),
        in_specs=[
            pl.BlockSpec(memory_space=pltpu.SMEM),
            pl.BlockSpec((1, WINDOW, qd), lambda b, c: (b, c, 0)),
            pl.BlockSpec((1, WINDOW, kvd2), lambda b, c: (b, jnp.maximum(c - 1, 0), 0)),
            pl.BlockSpec((1, WINDOW, kvd2), lambda b, c: (b, c, 0)),
            pl.BlockSpec((n_heads, WINDOW, 2 * WINDOW), lambda b, c: (0, 0, 0)),
        ],
        out_specs=pl.BlockSpec((1, WINDOW, qd), lambda b, c: (b, c, 0)),
        out_shape=jax.ShapeDtypeStruct((batch, seq, qd), BF16),
        compiler_params=_params(),
        name="swa_attention",
    )(sinks, q3, kv3, kv3, bias)
    return out.reshape(t, qd)


def _t5_bucket(dist):
    max_exact = N_BUCKETS // 2
    is_small = dist < max_exact
    d = jnp.maximum(dist, 1).astype(F32)
    large = max_exact + (jnp.log(d / max_exact) / math.log(MAX_DISTANCE / max_exact)
                         * (N_BUCKETS - max_exact)).astype(jnp.int32)
    large = jnp.minimum(large, N_BUCKETS - 1)
    return jnp.where(is_small, dist, large)


def _band_bias(rel_bias):
    dist = np.arange(WINDOW)[:, None] + WINDOW - np.arange(2 * WINDOW)[None, :]
    in_window = (dist >= 0) & (dist < WINDOW)
    bucket = _t5_bucket(jnp.asarray(np.clip(dist, 0, None), dtype=jnp.int32))
    bias = rel_bias[bucket].astype(F32).transpose(2, 0, 1)
    return jnp.where(jnp.asarray(in_window)[None], bias, NEG_INF)


def kernel(x, mix_norm, ffn_norm, a_w_in, a_norm_v, a_w_s, a_b_s, a_w_out, kv_norm, w_kv, b_kv, b_w_q, b_b_q, b_sinks, b_w_o, b_b_o, rel_bias, ffn_w_gate, ffn_w_up, ffn_w_down, final_norm):
    batch, seq, d = x.shape
    depth = mix_norm.shape[0]
    n_a = a_w_in.shape[0]
    h = x.reshape(batch * seq, d)
    bias = _band_bias(rel_bias)
    q_scale = 1.0 / math.sqrt(HEAD_DIM)
    kv = None
    for layer in range(depth):
        if layer < n_a:
            i = layer
            gated = _gmlp_in(h, mix_norm[layer], a_w_in[i].astype(BF16), a_norm_v[i], a_w_s[i], a_b_s[i])
            h = _linear(gated, a_w_out[i].astype(BF16), residual=h, out_dtype=F32, name="gmlp_out")
        else:
            i = layer - n_a
            q = _linear(h, b_w_q[i].astype(BF16), gain=mix_norm[layer], bias=b_b_q[i], scale=q_scale,
                        out_dtype=BF16, name="q_proj")
            attn = _attention(q, kv, b_sinks[i], bias, batch=batch, seq=seq)
            h = _linear(attn, b_w_o[i].astype(BF16), bias=b_b_o[i], residual=h, out_dtype=F32, name="o_proj")
        h = _ffn(h, ffn_norm[layer], ffn_w_gate[layer].astype(BF16), ffn_w_up[layer].astype(BF16),
                 ffn_w_down[layer].astype(BF16), final_gain=final_norm if layer == depth - 1 else None)
        if layer == n_a - 1:
            kv = _linear(h, w_kv.astype(BF16), gain=kv_norm, bias=b_kv, out_dtype=BF16, name="kv_proj")
    return h.reshape(batch, seq, d)
```

```python
import functools
import math

import jax
import jax.numpy as jnp
import numpy as np
from jax import lax
from jax.experimental import pallas as pl
from jax.experimental.pallas import tpu as pltpu

F32 = jnp.float32
BF16 = jnp.bfloat16

RMS_EPS = 1e-5
NEG_INF = -1e30

CHUNK = 128
A_GROUPS = 8
HEAD_DIM = 64
KV_GROUP = 8
WINDOW = 128
N_BUCKETS = 32
MAX_DISTANCE = 128

V7X_VMEM_BYTES = 64 * 1024 * 1024
VMEM_LIMIT_BYTES = V7X_VMEM_BYTES - 8 * 1024 * 1024

TM_LINEAR = 1024
TN_LINEAR = 1024
TM_GMLP = 1024
TN_GMLP = 512
TM_FFN = 512
TF_FFN = 512
NORM_ROWS = 256


def _params():
    return pltpu.CompilerParams(
        dimension_semantics=("arbitrary", "arbitrary"),
        vmem_limit_bytes=VMEM_LIMIT_BYTES,
    )


def _rmsnorm_rows(x_ref, g_ref, o_ref):
    rows = x_ref.shape[0]
    g = g_ref[...]

    def body(i, carry):
        r = pl.ds(pl.multiple_of(i * NORM_ROWS, NORM_ROWS), NORM_ROWS)
        x = x_ref[r, :]
        ms = jnp.mean(x * x, axis=-1, keepdims=True)
        o_ref[r, :] = ((x * lax.rsqrt(ms + RMS_EPS)) * g).astype(o_ref.dtype)
        return carry

    lax.fori_loop(0, rows // NORM_ROWS, body, 0)


def _linear_kernel(*refs, norm, bias, res, scale):
    refs = list(refs)
    x_ref = refs.pop(0)
    g_ref = refs.pop(0) if norm else None
    w_ref = refs.pop(0)
    b_ref = refs.pop(0) if bias else None
    r_ref = refs.pop(0) if res else None
    o_ref = refs.pop(0)
    xn_ref = refs.pop(0) if norm else None

    if norm:
        @pl.when(pl.program_id(1) == 0)
        def _():
            _rmsnorm_rows(x_ref, g_ref, xn_ref)
        lhs = xn_ref[...]
    else:
        lhs = x_ref[...]
    acc = jnp.dot(lhs, w_ref[...], preferred_element_type=F32)
    if bias:
        acc = acc + b_ref[...]
    if scale != 1.0:
        acc = acc * scale
    if res:
        acc = acc + r_ref[...]
    o_ref[...] = acc.astype(o_ref.dtype)


def _linear(x, w, *, gain=None, bias=None, residual=None, scale=1.0, out_dtype, name):
    t, k = x.shape
    n = w.shape[1]
    tm = min(TM_LINEAR, t)
    tn = min(TN_LINEAR, n)
    norm = gain is not None
    in_specs = [pl.BlockSpec((tm, k), lambda m, j: (m, 0))]
    args = [x]
    if norm:
        in_specs.append(pl.BlockSpec((1, k), lambda m, j: (0, 0)))
        args.append(gain.reshape(1, k))
    in_specs.append(pl.BlockSpec((k, tn), lambda m, j: (0, j)))
    args.append(w)
    if bias is not None:
        in_specs.append(pl.BlockSpec((1, tn), lambda m, j: (0, j)))
        args.append(bias.reshape(1, n))
    if residual is not None:
        in_specs.append(pl.BlockSpec((tm, tn), lambda m, j: (m, j)))
        args.append(residual)
    return pl.pallas_call(
        functools.partial(_linear_kernel, norm=norm, bias=bias is not None,
                          res=residual is not None, scale=scale),
        grid=(t // tm, n // tn),
        in_specs=in_specs,
        out_specs=pl.BlockSpec((tm, tn), lambda m, j: (m, j)),
        out_shape=jax.ShapeDtypeStruct((t, n), out_dtype),
        scratch_shapes=[pltpu.VMEM((tm, k), BF16)] if norm else [],
        compiler_params=_params(),
        name=name,
    )(*args)


def _gmlp_in_kernel(h_ref, g_ref, w_ref, nv_ref, ws_ref, bs_ref, o_ref,
                    xn_ref, vbuf_ref, vn_ref, wt_ref, *, nj):
    j = pl.program_id(1)
    tm = h_ref.shape[0]
    tn = w_ref.shape[1]
    gw = vn_ref.shape[2]
    gpt = tn // gw

    @pl.when(j == 0)
    def _():
        _rmsnorm_rows(h_ref, g_ref, xn_ref)

    z = jax.nn.gelu(jnp.dot(xn_ref[...], w_ref[...], preferred_element_type=F32))

    @pl.when(j < nj)
    def _():
        vbuf_ref[j] = z

    @pl.when(j == nj)
    def _():
        ssq = jnp.zeros((tm, 1), F32)
        for jj in range(nj):
            v = vbuf_ref[jj]
            ssq = ssq + jnp.sum(v * v, axis=-1, keepdims=True)
        rinv = lax.rsqrt(ssq / (nj * tn) + RMS_EPS)
        for g in range(nj * gpt):
            v = vbuf_ref[g // gpt][:, (g % gpt) * gw:(g % gpt + 1) * gw]
            nv = nv_ref[:, g * gw:(g + 1) * gw]
            vn_ref[g] = ((v * rinv) * nv).astype(BF16)
        row = lax.broadcasted_iota(jnp.int32, (CHUNK, CHUNK), 0)
        col = lax.broadcasted_iota(jnp.int32, (CHUNK, CHUNK), 1)
        for g in range(nj * gpt):
            wt_ref[g] = jnp.where(col <= row, ws_ref[g], 0.0).astype(BF16)

    @pl.when(j >= nj)
    def _():
        ju = j - nj
        for gl in range(gpt):
            g = ju * gpt + gl
            wt = wt_ref[g]
            b = bs_ref[g]
            for c in range(tm // CHUNK):
                rows = slice(c * CHUNK, (c + 1) * CHUNK)
                s = jnp.dot(wt, vn_ref[g, rows, :], preferred_element_type=F32) + b
                u = z[rows, gl * gw:(gl + 1) * gw]
                o_ref[rows, gl * gw:(gl + 1) * gw] = (u * s).astype(o_ref.dtype)


def _gmlp_in(h, gain, w_in, norm_v, w_s, b_s):
    t, d = h.shape
    half = w_in.shape[1] // 2
    groups = w_s.shape[0]
    gw = half // groups
    tm, tn = TM_GMLP, TN_GMLP
    nj = half // tn
    return pl.pallas_call(
        functools.partial(_gmlp_in_kernel, nj=nj),
        grid=(t // tm, 2 * nj),
        in_specs=[
            pl.BlockSpec((tm, d), lambda m, j: (m, 0)),
            pl.BlockSpec((1, d), lambda m, j: (0, 0)),
            pl.BlockSpec((d, tn), lambda m, j: (0, (j + nj) % (2 * nj))),
            pl.BlockSpec((1, half), lambda m, j: (0, 0)),
            pl.BlockSpec((groups, CHUNK, CHUNK), lambda m, j: (0, 0, 0)),
            pl.BlockSpec((groups, CHUNK, 1), lambda m, j: (0, 0, 0)),
        ],
        out_specs=pl.BlockSpec((tm, tn), lambda m, j: (m, jnp.maximum(j - nj, 0))),
        out_shape=jax.ShapeDtypeStruct((t, half), BF16),
        scratch_shapes=[
            pltpu.VMEM((tm, d), BF16),
            pltpu.VMEM((nj, tm, tn), F32),
            pltpu.VMEM((groups, tm, gw), BF16),
            pltpu.VMEM((groups, CHUNK, CHUNK), BF16),
        ],
        compiler_params=_params(),
        name="gmlp_in",
    )(h, gain.reshape(1, d), w_in, norm_v.reshape(1, half), w_s, b_s.reshape(groups, CHUNK, 1))


def _ffn_kernel(*refs, final):
    refs = list(refs)
    h_ref, g_ref, wg_ref, wu_ref, wd_ref = refs[:5]
    fg_ref = refs[5] if final else None
    o_ref, xn_ref = refs[-2:]
    f = pl.program_id(1)
    d = o_ref.shape[1]
    tn = wg_ref.shape[1]

    @pl.when(f == 0)
    def _():
        _rmsnorm_rows(h_ref, g_ref, xn_ref)
        o_ref[...] = h_ref[...]

    xn = xn_ref[...]
    gate = jnp.dot(xn, wg_ref[...], preferred_element_type=F32)
    up = jnp.dot(xn, wu_ref[...], preferred_element_type=F32)
    a = (jax.nn.silu(gate) * up).astype(BF16)
    for n in range(d // tn):
        cols = slice(n * tn, (n + 1) * tn)
        o_ref[:, cols] += jnp.dot(a, wd_ref[:, cols], preferred_element_type=F32)

    if final:
        @pl.when(f == pl.num_programs(1) - 1)
        def _():
            _rmsnorm_rows(o_ref, fg_ref, o_ref)


def _ffn(h, gain, wg, wu, wd, final_gain=None):
    t, d = h.shape
    ff = wg.shape[1]
    tm, tf = TM_FFN, TF_FFN
    final = final_gain is not None
    in_specs = [
        pl.BlockSpec((tm, d), lambda m, f: (m, 0)),
        pl.BlockSpec((1, d), lambda m, f: (0, 0)),
        pl.BlockSpec((d, tf), lambda m, f: (0, f)),
        pl.BlockSpec((d, tf), lambda m, f: (0, f)),
        pl.BlockSpec((tf, d), lambda m, f: (f, 0)),
    ]
    args = [h, gain.reshape(1, d), wg, wu, wd]
    if final:
        in_specs.append(pl.BlockSpec((1, d), lambda m, f: (0, 0)))
        args.append(final_gain.reshape(1, d))
    return pl.pallas_call(
        functools.partial(_ffn_kernel, final=final),
        grid=(t // tm, ff // tf),
        in_specs=in_specs,
        out_specs=pl.BlockSpec((tm, d), lambda m, f: (m, 0)),
        out_shape=jax.ShapeDtypeStruct((t, d), F32),
        scratch_shapes=[pltpu.VMEM((tm, d), BF16)],
        compiler_params=_params(),
        name="ffn",
    )(*args)


LANES = 128


def _attn_kernel(sink_ref, q_ref, kvp_ref, kvc_ref, bias_ref, o_ref, p_ref, *, n_kv):
    c = pl.program_id(1)
    w = WINDOW
    pairs = KV_GROUP * HEAD_DIM // LANES
    low2 = lax.broadcasted_iota(jnp.int32, (2 * w, LANES), 1) < HEAD_DIM
    low1 = lax.broadcasted_iota(jnp.int32, (w, LANES), 1) < HEAD_DIM
    col = lax.broadcasted_iota(jnp.int32, (w, 2 * w), 1)
    no_prev = jnp.logical_and(col < w, c == 0)
    zero = jnp.zeros((2 * w, LANES), BF16)

    def block_diag(prev, cur):
        x = jnp.concatenate([prev, cur], axis=0)
        return jnp.concatenate([jnp.where(low2, x, zero), jnp.where(low2, zero, x)], axis=0)

    for hk in range(n_kv):
        kcols = slice(hk * LANES, (hk + 1) * LANES)
        vcols = slice((n_kv + hk) * LANES, (n_kv + hk + 1) * LANES)
        kk = block_diag(kvp_ref[0, :, kcols], kvc_ref[0, :, kcols])
        vv = block_diag(kvp_ref[0, :, vcols], kvc_ref[0, :, vcols])
        qg = jnp.concatenate(
            [q_ref[0, :, (hk * pairs + p) * LANES:(hk * pairs + p + 1) * LANES] for p in range(pairs)], axis=0)
        s = lax.dot_general(qg, kk, (((1,), (1,)), ((), ())), preferred_element_type=F32)
        rinv = []
        for p in range(pairs):
            rows = slice(p * w, (p + 1) * w)
            for e in range(2):
                cols = slice(e * 2 * w, (e + 1) * 2 * w)
                sink = sink_ref[hk * KV_GROUP + 2 * p + e]
                sc = jnp.where(no_prev, NEG_INF, s[rows, cols] + bias_ref[hk, rows, cols])
                m = jnp.maximum(jnp.max(sc, axis=-1, keepdims=True), sink)
                pe = jnp.exp(sc - m)
                denom = jnp.sum(pe, axis=-1, keepdims=True) + jnp.exp(sink - m)
                p_ref[hk, rows, cols] = pe.astype(BF16)
                rinv.append(1.0 / denom)
        o = jnp.dot(p_ref[hk], vv, preferred_element_type=F32)
        for p in range(pairs):
            scale = jnp.where(low1, rinv[2 * p], rinv[2 * p + 1])
            ocols = slice((hk * pairs + p) * LANES, (hk * pairs + p + 1) * LANES)
            o_ref[0, :, ocols] = (o[p * w:(p + 1) * w] * scale).astype(o_ref.dtype)


def _attention(q, kv2, sinks, bias, *, batch, seq):
    t, qd = q.shape
    kvw = kv2.shape[1]
    n_kv = kvw // (2 * LANES)
    pairs = KV_GROUP * HEAD_DIM // LANES
    nb = seq // WINDOW
    q3 = q.reshape(batch, seq, qd)
    kv3 = kv2.reshape(batch, seq, kvw)
    out = pl.pallas_call(
        functools.partial(_attn_kernel, n_kv=n_kv),
        grid=(batch, nb),
        in_specs=[
            pl.BlockSpec(memory_space=pltpu.SMEM),
            pl.BlockSpec((1, WINDOW, qd), lambda b, c: (b, c, 0)),
            pl.BlockSpec((1, WINDOW, kvw), lambda b, c: (b, jnp.maximum(c - 1, 0), 0)),
            pl.BlockSpec((1, WINDOW, kvw), lambda b, c: (b, c, 0)),
            pl.BlockSpec((n_kv, pairs * WINDOW, 4 * WINDOW), lambda b, c: (0, 0, 0)),
        ],
        out_specs=pl.BlockSpec((1, WINDOW, qd), lambda b, c: (b, c, 0)),
        out_shape=jax.ShapeDtypeStruct((batch, seq, qd), BF16),
        scratch_shapes=[pltpu.VMEM((n_kv, pairs * WINDOW, 4 * WINDOW), BF16)],
        compiler_params=_params(),
        name="swa_attention",
    )(sinks, q3, kv3, kv3, bias)
    return out.reshape(t, qd)


def _t5_bucket(dist):
    max_exact = N_BUCKETS // 2
    is_small = dist < max_exact
    d = jnp.maximum(dist, 1).astype(F32)
    large = max_exact + (jnp.log(d / max_exact) / math.log(MAX_DISTANCE / max_exact)
                         * (N_BUCKETS - max_exact)).astype(jnp.int32)
    large = jnp.minimum(large, N_BUCKETS - 1)
    return jnp.where(is_small, dist, large)


def _band_bias(rel_bias, n_kv):
    w = WINDOW
    n_heads = rel_bias.shape[1]
    by_dist = rel_bias[_t5_bucket(jnp.arange(w, dtype=jnp.int32))].astype(F32).T
    neg = jnp.full((n_heads, 1), NEG_INF, F32)
    r = jnp.concatenate([neg, by_dist[:, ::-1], jnp.broadcast_to(neg, (n_heads, w))], axis=1)
    band = jnp.tile(r, (1, w))[:, :w * 2 * w].reshape(n_heads, w, 2 * w)
    pairs = n_heads // n_kv // 2
    band = band.reshape(n_kv, pairs, 2, w, 2 * w).transpose(0, 1, 3, 2, 4)
    return band.reshape(n_kv, pairs * w, 4 * w)


def _duplicate_heads(a):
    lead = a.shape[:-1]
    n = a.shape[-1] // HEAD_DIM
    a = a.reshape(*lead, n, 1, HEAD_DIM)
    return jnp.broadcast_to(a, (*lead, n, LANES // HEAD_DIM, HEAD_DIM)).reshape(*lead, n * LANES)


def kernel(x, mix_norm, ffn_norm, a_w_in, a_norm_v, a_w_s, a_b_s, a_w_out, kv_norm, w_kv, b_kv, b_w_q, b_b_q, b_sinks, b_w_o, b_b_o, rel_bias, ffn_w_gate, ffn_w_up, ffn_w_down, final_norm):
    batch, seq, d = x.shape
    depth = mix_norm.shape[0]
    n_a = a_w_in.shape[0]
    h = x.reshape(batch * seq, d)
    n_kv = w_kv.shape[1] // (2 * HEAD_DIM)
    bias = _band_bias(rel_bias, n_kv)
    q_scale = 1.0 / math.sqrt(HEAD_DIM)
    kv = None
    for layer in range(depth):
        if layer < n_a:
            i = layer
            gated = _gmlp_in(h, mix_norm[layer], a_w_in[i].astype(BF16), a_norm_v[i], a_w_s[i], a_b_s[i])
            h = _linear(gated, a_w_out[i].astype(BF16), residual=h, out_dtype=F32, name="gmlp_out")
        else:
            i = layer - n_a
            q = _linear(h, b_w_q[i].astype(BF16), gain=mix_norm[layer], bias=b_b_q[i], scale=q_scale,
                        out_dtype=BF16, name="q_proj")
            attn = _attention(q, kv, b_sinks[i], bias, batch=batch, seq=seq)
            h = _linear(attn, b_w_o[i].astype(BF16), bias=b_b_o[i], residual=h, out_dtype=F32, name="o_proj")
        h = _ffn(h, ffn_norm[layer], ffn_w_gate[layer].astype(BF16), ffn_w_up[layer].astype(BF16),
                 ffn_w_down[layer].astype(BF16), final_gain=final_norm if layer == depth - 1 else None)
        if layer == n_a - 1:
            kv = _linear(h, _duplicate_heads(w_kv).astype(BF16), gain=kv_norm, bias=_duplicate_heads(b_kv),
                         out_dtype=BF16, name="kv_proj")
    return h.reshape(batch, seq, d)
```

```python
import functools
import math

import jax
import jax.numpy as jnp
import numpy as np
from jax import lax
from jax.experimental import pallas as pl
from jax.experimental.pallas import tpu as pltpu

F32 = jnp.float32
BF16 = jnp.bfloat16

RMS_EPS = 1e-5
NEG_INF = -1e30

CHUNK = 128
A_GROUPS = 8
HEAD_DIM = 64
KV_GROUP = 8
WINDOW = 128
N_BUCKETS = 32
MAX_DISTANCE = 128

V7X_VMEM_BYTES = 64 * 1024 * 1024
VMEM_LIMIT_BYTES = V7X_VMEM_BYTES - 8 * 1024 * 1024

TM_LINEAR = 1024
TN_LINEAR = 1024
TM_GMLP = 1024
TN_GMLP = 512
TM_FFN = 1024
TF_FFN = 256
FFN_DOWN_COLS = 512
NORM_ROWS = 256


def _params():
    return pltpu.CompilerParams(
        dimension_semantics=("arbitrary", "arbitrary"),
        vmem_limit_bytes=VMEM_LIMIT_BYTES,
    )


def _rmsnorm_rows(x_ref, g_ref, o_ref):
    rows = x_ref.shape[0]
    g = g_ref[...]

    def body(i, carry):
        r = pl.ds(pl.multiple_of(i * NORM_ROWS, NORM_ROWS), NORM_ROWS)
        x = x_ref[r, :]
        ms = jnp.mean(x * x, axis=-1, keepdims=True)
        o_ref[r, :] = ((x * lax.rsqrt(ms + RMS_EPS)) * g).astype(o_ref.dtype)
        return carry

    lax.fori_loop(0, rows // NORM_ROWS, body, 0)


def _linear_kernel(*refs, norm, bias, res, scale):
    refs = list(refs)
    x_ref = refs.pop(0)
    g_ref = refs.pop(0) if norm else None
    w_ref = refs.pop(0)
    b_ref = refs.pop(0) if bias else None
    r_ref = refs.pop(0) if res else None
    o_ref = refs.pop(0)
    keep_ref = refs.pop(0)

    @pl.when(pl.program_id(1) == 0)
    def _():
        if norm:
            _rmsnorm_rows(x_ref, g_ref, keep_ref)
        else:
            keep_ref[...] = w_ref[...].astype(BF16)

    if norm:
        acc = jnp.dot(keep_ref[...], w_ref[...].astype(BF16), preferred_element_type=F32)
    else:
        acc = jnp.dot(x_ref[...], keep_ref[...], preferred_element_type=F32)
    if bias:
        acc = acc + b_ref[...]
    if scale != 1.0:
        acc = acc * scale
    if res:
        acc = acc + r_ref[...]
    o_ref[...] = acc.astype(o_ref.dtype)


def _linear(x, w, layer, *, gain=None, bias=None, residual=None, scale=1.0, out_dtype, name):
    t, k = x.shape
    n = w.shape[2]
    tm = min(TM_LINEAR, t)
    tn = min(TN_LINEAR, n)
    norm = gain is not None
    if norm:
        grid = (t // tm, n // tn)
        row, colj = (lambda a, b: a), (lambda a, b: b)
    else:
        grid = (n // tn, t // tm)
        row, colj = (lambda a, b: b), (lambda a, b: a)
    in_specs = [pl.BlockSpec((tm, k), lambda a, b: (row(a, b), 0))]
    args = [x]
    if norm:
        in_specs.append(pl.BlockSpec((1, k), lambda a, b: (0, 0)))
        args.append(gain.reshape(1, k))
    in_specs.append(pl.BlockSpec((None, k, tn), lambda a, b: (layer, 0, colj(a, b))))
    args.append(w)
    if bias is not None:
        in_specs.append(pl.BlockSpec((1, tn), lambda a, b: (0, colj(a, b))))
        args.append(bias.reshape(1, n))
    if residual is not None:
        in_specs.append(pl.BlockSpec((tm, tn), lambda a, b: (row(a, b), colj(a, b))))
        args.append(residual)
    return pl.pallas_call(
        functools.partial(_linear_kernel, norm=norm, bias=bias is not None,
                          res=residual is not None, scale=scale),
        grid=grid,
        in_specs=in_specs,
        out_specs=pl.BlockSpec((tm, tn), lambda a, b: (row(a, b), colj(a, b))),
        out_shape=jax.ShapeDtypeStruct((t, n), out_dtype),
        scratch_shapes=[pltpu.VMEM((tm, k) if norm else (k, tn), BF16)],
        compiler_params=_params(),
        name=name,
    )(*args)


def _gmlp_in_kernel(h_ref, g_ref, w_ref, nv_ref, ws_ref, bs_ref, o_ref,
                    xn_ref, vbuf_ref, vn_ref, wt_ref, *, nj):
    j = pl.program_id(1)
    tm = h_ref.shape[0]
    tn = w_ref.shape[1]
    gw = vn_ref.shape[2]
    gpt = tn // gw

    @pl.when(j == 0)
    def _():
        _rmsnorm_rows(h_ref, g_ref, xn_ref)

    z = jax.nn.gelu(jnp.dot(xn_ref[...], w_ref[...].astype(BF16), preferred_element_type=F32))

    @pl.when(j < nj)
    def _():
        vbuf_ref[j] = z

    @pl.when(j == nj)
    def _():
        ssq = jnp.zeros((tm, 1), F32)
        for jj in range(nj):
            v = vbuf_ref[jj]
            ssq = ssq + jnp.sum(v * v, axis=-1, keepdims=True)
        rinv = lax.rsqrt(ssq / (nj * tn) + RMS_EPS)
        for g in range(nj * gpt):
            v = vbuf_ref[g // gpt][:, (g % gpt) * gw:(g % gpt + 1) * gw]
            nv = nv_ref[:, g * gw:(g + 1) * gw]
            vn_ref[g] = ((v * rinv) * nv).astype(BF16)
        row = lax.broadcasted_iota(jnp.int32, (CHUNK, CHUNK), 0)
        col = lax.broadcasted_iota(jnp.int32, (CHUNK, CHUNK), 1)
        for g in range(nj * gpt):
            wt_ref[g] = jnp.where(col <= row, ws_ref[g], 0.0).astype(BF16)

    @pl.when(j >= nj)
    def _():
        ju = j - nj
        for gl in range(gpt):
            g = ju * gpt + gl
            wt = wt_ref[g]
            b = bs_ref[g]
            for c in range(tm // CHUNK):
                rows = slice(c * CHUNK, (c + 1) * CHUNK)
                s = jnp.dot(wt, vn_ref[g, rows, :], preferred_element_type=F32) + b
                u = z[rows, gl * gw:(gl + 1) * gw]
                o_ref[rows, gl * gw:(gl + 1) * gw] = (u * s).astype(o_ref.dtype)


def _gmlp_in(h, gain, w_in, layer, norm_v, w_s, b_s):
    t, d = h.shape
    half = w_in.shape[2] // 2
    groups = w_s.shape[0]
    gw = half // groups
    tm, tn = TM_GMLP, TN_GMLP
    nj = half // tn
    return pl.pallas_call(
        functools.partial(_gmlp_in_kernel, nj=nj),
        grid=(t // tm, 2 * nj),
        in_specs=[
            pl.BlockSpec((tm, d), lambda m, j: (m, 0)),
            pl.BlockSpec((1, d), lambda m, j: (0, 0)),
            pl.BlockSpec((None, d, tn), lambda m, j: (layer, 0, (j + nj) % (2 * nj))),
            pl.BlockSpec((1, half), lambda m, j: (0, 0)),
            pl.BlockSpec((groups, CHUNK, CHUNK), lambda m, j: (0, 0, 0)),
            pl.BlockSpec((groups, CHUNK, 1), lambda m, j: (0, 0, 0)),
        ],
        out_specs=pl.BlockSpec((tm, tn), lambda m, j: (m, jnp.maximum(j - nj, 0))),
        out_shape=jax.ShapeDtypeStruct((t, half), BF16),
        scratch_shapes=[
            pltpu.VMEM((tm, d), BF16),
            pltpu.VMEM((nj, tm, tn), F32),
            pltpu.VMEM((groups, tm, gw), BF16),
            pltpu.VMEM((groups, CHUNK, CHUNK), BF16),
        ],
        compiler_params=_params(),
        name="gmlp_in",
    )(h, gain.reshape(1, d), w_in, norm_v.reshape(1, half), w_s, b_s.reshape(groups, CHUNK, 1))


def _ffn_kernel(*refs, final):
    refs = list(refs)
    h_ref, g_ref, wg_ref, wu_ref, wd_ref = refs[:5]
    fg_ref = refs[5] if final else None
    o_ref, xn_ref = refs[-2:]
    f = pl.program_id(1)
    d = o_ref.shape[1]

    @pl.when(f == 0)
    def _():
        _rmsnorm_rows(h_ref, g_ref, xn_ref)
        o_ref[...] = h_ref[...]

    xn = xn_ref[...]
    gate = jnp.dot(xn, wg_ref[...].astype(BF16), preferred_element_type=F32)
    up = jnp.dot(xn, wu_ref[...].astype(BF16), preferred_element_type=F32)
    a = (jax.nn.silu(gate) * up).astype(BF16)
    for n in range(d // FFN_DOWN_COLS):
        cols = slice(n * FFN_DOWN_COLS, (n + 1) * FFN_DOWN_COLS)
        o_ref[:, cols] += jnp.dot(a, wd_ref[:, cols].astype(BF16), preferred_element_type=F32)

    if final:
        @pl.when(f == pl.num_programs(1) - 1)
        def _():
            _rmsnorm_rows(o_ref, fg_ref, o_ref)


def _ffn(h, gain, wg, wu, wd, layer, final_gain=None):
    t, d = h.shape
    ff = wg.shape[2]
    tm, tf = TM_FFN, TF_FFN
    final = final_gain is not None
    in_specs = [
        pl.BlockSpec((tm, d), lambda m, f: (m, 0), pipeline_mode=pl.Buffered(1)),
        pl.BlockSpec((1, d), lambda m, f: (0, 0)),
        pl.BlockSpec((None, d, tf), lambda m, f: (layer, 0, f)),
        pl.BlockSpec((None, d, tf), lambda m, f: (layer, 0, f)),
        pl.BlockSpec((None, tf, d), lambda m, f: (layer, f, 0)),
    ]
    args = [h, gain.reshape(1, d), wg, wu, wd]
    if final:
        in_specs.append(pl.BlockSpec((1, d), lambda m, f: (0, 0)))
        args.append(final_gain.reshape(1, d))
    return pl.pallas_call(
        functools.partial(_ffn_kernel, final=final),
        grid=(t // tm, ff // tf),
        in_specs=in_specs,
        out_specs=pl.BlockSpec((tm, d), lambda m, f: (m, 0)),
        out_shape=jax.ShapeDtypeStruct((t, d), F32),
        scratch_shapes=[pltpu.VMEM((tm, d), BF16)],
        compiler_params=_params(),
        name="ffn",
    )(*args)


LANES = 128


def _attn_kernel(sink_ref, q_ref, kvp_ref, kvc_ref, bias_ref, o_ref, p_ref, *, n_kv):
    c = pl.program_id(1)
    w = WINDOW
    pairs = KV_GROUP * HEAD_DIM // LANES
    low2 = lax.broadcasted_iota(jnp.int32, (2 * w, LANES), 1) < HEAD_DIM
    low1 = lax.broadcasted_iota(jnp.int32, (w, LANES), 1) < HEAD_DIM
    col = lax.broadcasted_iota(jnp.int32, (w, 2 * w), 1)
    no_prev = jnp.logical_and(col < w, c == 0)
    zero = jnp.zeros((2 * w, LANES), BF16)

    def block_diag(prev, cur):
        x = jnp.concatenate([prev, cur], axis=0)
        return jnp.concatenate([jnp.where(low2, x, zero), jnp.where(low2, zero, x)], axis=0)

    for hk in range(n_kv):
        kcols = slice(hk * LANES, (hk + 1) * LANES)
        vcols = slice((n_kv + hk) * LANES, (n_kv + hk + 1) * LANES)
        kk = block_diag(kvp_ref[0, :, kcols], kvc_ref[0, :, kcols])
        vv = block_diag(kvp_ref[0, :, vcols], kvc_ref[0, :, vcols])
        qg = jnp.concatenate(
            [q_ref[0, :, (hk * pairs + p) * LANES:(hk * pairs + p + 1) * LANES] for p in range(pairs)], axis=0)
        s = lax.dot_general(qg, kk, (((1,), (1,)), ((), ())), preferred_element_type=F32)
        rinv = []
        for p in range(pairs):
            rows = slice(p * w, (p + 1) * w)
            for e in range(2):
                cols = slice(e * 2 * w, (e + 1) * 2 * w)
                sink = sink_ref[hk * KV_GROUP + 2 * p + e]
                sc = jnp.where(no_prev, NEG_INF, s[rows, cols] + bias_ref[hk, rows, cols])
                m = jnp.maximum(jnp.max(sc, axis=-1, keepdims=True), sink)
                pe = jnp.exp(sc - m)
                denom = jnp.sum(pe, axis=-1, keepdims=True) + jnp.exp(sink - m)
                p_ref[hk, rows, cols] = pe.astype(BF16)
                rinv.append(1.0 / denom)
        o = jnp.dot(p_ref[hk], vv, preferred_element_type=F32)
        for p in range(pairs):
            scale = jnp.where(low1, rinv[2 * p], rinv[2 * p + 1])
            ocols = slice((hk * pairs + p) * LANES, (hk * pairs + p + 1) * LANES)
            o_ref[0, :, ocols] = (o[p * w:(p + 1) * w] * scale).astype(o_ref.dtype)


def _attention(q, kv2, sinks, bias, *, batch, seq):
    t, qd = q.shape
    kvw = kv2.shape[1]
    n_kv = kvw // (2 * LANES)
    pairs = KV_GROUP * HEAD_DIM // LANES
    nb = seq // WINDOW
    q3 = q.reshape(batch, seq, qd)
    kv3 = kv2.reshape(batch, seq, kvw)
    out = pl.pallas_call(
        functools.partial(_attn_kernel, n_kv=n_kv),
        grid=(batch, nb),
        in_specs=[
            pl.BlockSpec(memory_space=pltpu.SMEM),
            pl.BlockSpec((1, WINDOW, qd), lambda b, c: (b, c, 0)),
            pl.BlockSpec((1, WINDOW, kvw), lambda b, c: (b, jnp.maximum(c - 1, 0), 0)),
            pl.BlockSpec((1, WINDOW, kvw), lambda b, c: (b, c, 0)),
            pl.BlockSpec((n_kv, pairs * WINDOW, 4 * WINDOW), lambda b, c: (0, 0, 0)),
        ],
        out_specs=pl.BlockSpec((1, WINDOW, qd), lambda b, c: (b, c, 0)),
        out_shape=jax.ShapeDtypeStruct((batch, seq, qd), BF16),
        scratch_shapes=[pltpu.VMEM((n_kv, pairs * WINDOW, 4 * WINDOW), BF16)],
        compiler_params=_params(),
        name="swa_attention",
    )(sinks, q3, kv3, kv3, bias)
    return out.reshape(t, qd)


def _t5_bucket(dist):
    max_exact = N_BUCKETS // 2
    is_small = dist < max_exact
    d = jnp.maximum(dist, 1).astype(F32)
    large = max_exact + (jnp.log(d / max_exact) / math.log(MAX_DISTANCE / max_exact)
                         * (N_BUCKETS - max_exact)).astype(jnp.int32)
    large = jnp.minimum(large, N_BUCKETS - 1)
    return jnp.where(is_small, dist, large)


def _band_bias(rel_bias, n_kv):
    w = WINDOW
    n_heads = rel_bias.shape[1]
    by_dist = rel_bias[_t5_bucket(jnp.arange(w, dtype=jnp.int32))].astype(F32).T
    neg = jnp.full((n_heads, 1), NEG_INF, F32)
    r = jnp.concatenate([neg, by_dist[:, ::-1], jnp.broadcast_to(neg, (n_heads, w))], axis=1)
    band = jnp.tile(r, (1, w))[:, :w * 2 * w].reshape(n_heads, w, 2 * w)
    pairs = n_heads // n_kv // 2
    band = band.reshape(n_kv, pairs, 2, w, 2 * w).transpose(0, 1, 3, 2, 4)
    return band.reshape(n_kv, pairs * w, 4 * w)


def _duplicate_heads(a):
    lead = a.shape[:-1]
    n = a.shape[-1] // HEAD_DIM
    a = a.reshape(*lead, n, 1, HEAD_DIM)
    return jnp.broadcast_to(a, (*lead, n, LANES // HEAD_DIM, HEAD_DIM)).reshape(*lead, n * LANES)


def kernel(x, mix_norm, ffn_norm, a_w_in, a_norm_v, a_w_s, a_b_s, a_w_out, kv_norm, w_kv, b_kv, b_w_q, b_b_q, b_sinks, b_w_o, b_b_o, rel_bias, ffn_w_gate, ffn_w_up, ffn_w_down, final_norm):
    batch, seq, d = x.shape
    depth = mix_norm.shape[0]
    n_a = a_w_in.shape[0]
    h = x.reshape(batch * seq, d)
    n_kv = w_kv.shape[1] // (2 * HEAD_DIM)
    bias = _band_bias(rel_bias, n_kv)
    q_scale = 1.0 / math.sqrt(HEAD_DIM)
    kv = None
    for layer in range(depth):
        if layer < n_a:
            i = layer
            gated = _gmlp_in(h, mix_norm[layer], a_w_in, i, a_norm_v[i], a_w_s[i], a_b_s[i])
            h = _linear(gated, a_w_out, i, residual=h, out_dtype=F32, name="gmlp_out")
        else:
            i = layer - n_a
            q = _linear(h, b_w_q, i, gain=mix_norm[layer], bias=b_b_q[i], scale=q_scale,
                        out_dtype=BF16, name="q_proj")
            attn = _attention(q, kv, b_sinks[i], bias, batch=batch, seq=seq)
            h = _linear(attn, b_w_o, i, bias=b_b_o[i], residual=h, out_dtype=F32, name="o_proj")
        h = _ffn(h, ffn_norm[layer], ffn_w_gate, ffn_w_up, ffn_w_down, layer,
                 final_gain=final_norm if layer == depth - 1 else None)
        if layer == n_a - 1:
            kv = _linear(h, _duplicate_heads(w_kv)[None], 0, gain=kv_norm, bias=_duplicate_heads(b_kv),
                         out_dtype=BF16, name="kv_proj")
    return h.reshape(batch, seq, d)
```

```python
import functools
import math

import jax
import jax.numpy as jnp
import numpy as np
from jax import lax
from jax.experimental import pallas as pl
from jax.experimental.pallas import tpu as pltpu

F32 = jnp.float32
BF16 = jnp.bfloat16

RMS_EPS = 1e-5
NEG_INF = -1e30

CHUNK = 128
A_GROUPS = 8
HEAD_DIM = 64
KV_GROUP = 8
WINDOW = 128
N_BUCKETS = 32
MAX_DISTANCE = 128

V7X_VMEM_BYTES = 64 * 1024 * 1024
VMEM_LIMIT_BYTES = V7X_VMEM_BYTES - 8 * 1024 * 1024

TM_LINEAR = 1024
TN_LINEAR = 1024
TM_GMLP = 1024
TN_GMLP = 1024
GMLP_ROWS = 512
TM_FFN = 1024
TF_FFN = 256
FFN_DOWN_COLS = 512
NORM_ROWS = 256


def _params():
    return pltpu.CompilerParams(
        dimension_semantics=("arbitrary", "arbitrary"),
        vmem_limit_bytes=VMEM_LIMIT_BYTES,
    )


def _rmsnorm_rows(x_ref, g_ref, o_ref):
    rows = x_ref.shape[0]
    g = g_ref[...]

    def body(i, carry):
        r = pl.ds(pl.multiple_of(i * NORM_ROWS, NORM_ROWS), NORM_ROWS)
        x = x_ref[r, :]
        ms = jnp.mean(x * x, axis=-1, keepdims=True)
        o_ref[r, :] = ((x * lax.rsqrt(ms + RMS_EPS)) * g).astype(o_ref.dtype)
        return carry

    lax.fori_loop(0, rows // NORM_ROWS, body, 0)


def _linear_kernel(*refs, norm, bias, res, scale):
    refs = list(refs)
    x_ref = refs.pop(0)
    g_ref = refs.pop(0) if norm else None
    w_ref = refs.pop(0)
    b_ref = refs.pop(0) if bias else None
    r_ref = refs.pop(0) if res else None
    o_ref = refs.pop(0)
    keep_ref = refs.pop(0)

    @pl.when(pl.program_id(1) == 0)
    def _():
        if norm:
            _rmsnorm_rows(x_ref, g_ref, keep_ref)
        else:
            keep_ref[...] = w_ref[...].astype(BF16)

    if norm:
        acc = jnp.dot(keep_ref[...], w_ref[...].astype(BF16), preferred_element_type=F32)
    else:
        acc = jnp.dot(x_ref[...], keep_ref[...], preferred_element_type=F32)
    if bias:
        acc = acc + b_ref[...]
    if scale != 1.0:
        acc = acc * scale
    if res:
        acc = acc + r_ref[...]
    o_ref[...] = acc.astype(o_ref.dtype)


def _linear(x, w, layer, *, gain=None, bias=None, residual=None, scale=1.0, out_dtype, name):
    t, k = x.shape
    n = w.shape[2]
    tm = min(TM_LINEAR, t)
    tn = min(TN_LINEAR, n)
    norm = gain is not None
    if norm:
        grid = (t // tm, n // tn)
        row, colj = (lambda a, b: a), (lambda a, b: b)
    else:
        grid = (n // tn, t // tm)
        row, colj = (lambda a, b: b), (lambda a, b: a)
    in_specs = [pl.BlockSpec((tm, k), lambda a, b: (row(a, b), 0))]
    args = [x]
    if norm:
        in_specs.append(pl.BlockSpec((1, k), lambda a, b: (0, 0)))
        args.append(gain.reshape(1, k))
    in_specs.append(pl.BlockSpec((None, k, tn), lambda a, b: (layer, 0, colj(a, b))))
    args.append(w)
    if bias is not None:
        in_specs.append(pl.BlockSpec((1, tn), lambda a, b: (0, colj(a, b))))
        args.append(bias.reshape(1, n))
    if residual is not None:
        in_specs.append(pl.BlockSpec((tm, tn), lambda a, b: (row(a, b), colj(a, b))))
        args.append(residual)
    return pl.pallas_call(
        functools.partial(_linear_kernel, norm=norm, bias=bias is not None,
                          res=residual is not None, scale=scale),
        grid=grid,
        in_specs=in_specs,
        out_specs=pl.BlockSpec((tm, tn), lambda a, b: (row(a, b), colj(a, b))),
        out_shape=jax.ShapeDtypeStruct((t, n), out_dtype),
        scratch_shapes=[pltpu.VMEM((tm, k) if norm else (k, tn), BF16)],
        compiler_params=_params(),
        name=name,
    )(*args)


def _gmlp_in_kernel(h_ref, g_ref, w_ref, nv_ref, ws_ref, bs_ref, o_ref,
                    xn_ref, vbuf_ref, ssq_ref, *, nj):
    j = pl.program_id(1)
    tm = h_ref.shape[0]
    tn = w_ref.shape[1]
    gw = nv_ref.shape[2]
    gpt = tn // gw

    @pl.when(j == 0)
    def _():
        _rmsnorm_rows(h_ref, g_ref, xn_ref)
        ssq_ref[...] = jnp.zeros_like(ssq_ref)

    def gelu_block(gl, r):
        w = w_ref[:, gl * gw:(gl + 1) * gw].astype(BF16)
        x = xn_ref[r * GMLP_ROWS:(r + 1) * GMLP_ROWS, :]
        return jax.nn.gelu(jnp.dot(x, w, preferred_element_type=F32))

    @pl.when(j < nj)
    def _():
        for r in range(tm // GMLP_ROWS):
            rows = slice(r * GMLP_ROWS, (r + 1) * GMLP_ROWS)
            ssq = ssq_ref[rows, :]
            for gl in range(gpt):
                v = gelu_block(gl, r)
                vbuf_ref[j, rows, gl * gw:(gl + 1) * gw] = v
                ssq = ssq + jnp.sum(v * v, axis=-1, keepdims=True)
            ssq_ref[rows, :] = ssq

    @pl.when(j >= nj)
    def _():
        ju = j - nj
        row = lax.broadcasted_iota(jnp.int32, (CHUNK, CHUNK), 0)
        col = lax.broadcasted_iota(jnp.int32, (CHUNK, CHUNK), 1)
        for r in range(tm // GMLP_ROWS):
            rows = slice(r * GMLP_ROWS, (r + 1) * GMLP_ROWS)
            rinv = lax.rsqrt(ssq_ref[rows, :] / (nj * tn) + RMS_EPS)
            for gl in range(gpt):
                g = ju * gpt + gl
                cols = slice(gl * gw, (gl + 1) * gw)
                u = gelu_block(gl, r)
                vn = ((vbuf_ref[ju, rows, cols] * rinv) * nv_ref[g]).astype(BF16)
                wt = jnp.where(col <= row, ws_ref[g], 0.0).astype(BF16)
                b = bs_ref[g]
                for c in range(GMLP_ROWS // CHUNK):
                    crows = slice(c * CHUNK, (c + 1) * CHUNK)
                    s = jnp.dot(wt, vn[crows], preferred_element_type=F32) + b
                    o_ref[r * GMLP_ROWS + c * CHUNK:r * GMLP_ROWS + (c + 1) * CHUNK, cols] = (
                        u[crows] * s).astype(o_ref.dtype)


def _gmlp_in(h, gain, w_in, layer, norm_v, w_s, b_s):
    t, d = h.shape
    half = w_in.shape[2] // 2
    groups = w_s.shape[0]
    gw = half // groups
    tm, tn = TM_GMLP, TN_GMLP
    nj = half // tn
    return pl.pallas_call(
        functools.partial(_gmlp_in_kernel, nj=nj),
        grid=(t // tm, 2 * nj),
        in_specs=[
            pl.BlockSpec((tm, d), lambda m, j: (m, 0)),
            pl.BlockSpec((1, d), lambda m, j: (0, 0)),
            pl.BlockSpec((None, d, tn), lambda m, j: (layer, 0, (j + nj) % (2 * nj))),
            pl.BlockSpec((groups, 1, gw), lambda m, j: (0, 0, 0)),
            pl.BlockSpec((groups, CHUNK, CHUNK), lambda m, j: (0, 0, 0)),
            pl.BlockSpec((groups, CHUNK, 1), lambda m, j: (0, 0, 0)),
        ],
        out_specs=pl.BlockSpec((tm, tn), lambda m, j: (m, jnp.maximum(j - nj, 0))),
        out_shape=jax.ShapeDtypeStruct((t, half), BF16),
        scratch_shapes=[
            pltpu.VMEM((tm, d), BF16),
            pltpu.VMEM((nj, tm, tn), F32),
            pltpu.VMEM((tm, 1), F32),
        ],
        compiler_params=_params(),
        name="gmlp_in",
    )(h, gain.reshape(1, d), w_in, norm_v.reshape(groups, 1, gw), w_s, b_s.reshape(groups, CHUNK, 1))


def _ffn_kernel(*refs, final):
    refs = list(refs)
    h_ref, g_ref, wg_ref, wu_ref, wd_ref = refs[:5]
    fg_ref = refs[5] if final else None
    o_ref, xn_ref = refs[-2:]
    f = pl.program_id(1)
    d = o_ref.shape[1]

    @pl.when(f == 0)
    def _():
        _rmsnorm_rows(h_ref, g_ref, xn_ref)
        o_ref[...] = h_ref[...]

    xn = xn_ref[...]
    gate = jnp.dot(xn, wg_ref[...].astype(BF16), preferred_element_type=F32)
    up = jnp.dot(xn, wu_ref[...].astype(BF16), preferred_element_type=F32)
    a = (jax.nn.silu(gate) * up).astype(BF16)
    for n in range(d // FFN_DOWN_COLS):
        cols = slice(n * FFN_DOWN_COLS, (n + 1) * FFN_DOWN_COLS)
        o_ref[:, cols] += jnp.dot(a, wd_ref[:, cols].astype(BF16), preferred_element_type=F32)

    if final:
        @pl.when(f == pl.num_programs(1) - 1)
        def _():
            _rmsnorm_rows(o_ref, fg_ref, o_ref)


def _ffn(h, gain, wg, wu, wd, layer, final_gain=None):
    t, d = h.shape
    ff = wg.shape[2]
    tm, tf = TM_FFN, TF_FFN
    final = final_gain is not None
    in_specs = [
        pl.BlockSpec((tm, d), lambda m, f: (m, 0)),
        pl.BlockSpec((1, d), lambda m, f: (0, 0)),
        pl.BlockSpec((None, d, tf), lambda m, f: (layer, 0, f)),
        pl.BlockSpec((None, d, tf), lambda m, f: (layer, 0, f)),
        pl.BlockSpec((None, tf, d), lambda m, f: (layer, f, 0)),
    ]
    args = [h, gain.reshape(1, d), wg, wu, wd]
    if final:
        in_specs.append(pl.BlockSpec((1, d), lambda m, f: (0, 0)))
        args.append(final_gain.reshape(1, d))
    return pl.pallas_call(
        functools.partial(_ffn_kernel, final=final),
        grid=(t // tm, ff // tf),
        in_specs=in_specs,
        out_specs=pl.BlockSpec((tm, d), lambda m, f: (m, 0)),
        out_shape=jax.ShapeDtypeStruct((t, d), F32),
        scratch_shapes=[pltpu.VMEM((tm, d), BF16)],
        compiler_params=_params(),
        name="ffn",
    )(*args)


LANES = 128


def _attn_kernel(sink_ref, q_ref, kvp_ref, kvc_ref, bias_ref, o_ref, p_ref, *, n_kv):
    c = pl.program_id(1)
    w = WINDOW
    pairs = KV_GROUP * HEAD_DIM // LANES
    low2 = lax.broadcasted_iota(jnp.int32, (2 * w, LANES), 1) < HEAD_DIM
    low1 = lax.broadcasted_iota(jnp.int32, (w, LANES), 1) < HEAD_DIM
    col = lax.broadcasted_iota(jnp.int32, (w, 2 * w), 1)
    no_prev = jnp.logical_and(col < w, c == 0)
    zero = jnp.zeros((2 * w, LANES), BF16)

    def block_diag(prev, cur):
        x = jnp.concatenate([prev, cur], axis=0)
        return jnp.concatenate([jnp.where(low2, x, zero), jnp.where(low2, zero, x)], axis=0)

    for hk in range(n_kv):
        kcols = slice(hk * LANES, (hk + 1) * LANES)
        vcols = slice((n_kv + hk) * LANES, (n_kv + hk + 1) * LANES)
        kk = block_diag(kvp_ref[0, :, kcols], kvc_ref[0, :, kcols])
        vv = block_diag(kvp_ref[0, :, vcols], kvc_ref[0, :, vcols])
        qg = jnp.concatenate(
            [q_ref[0, :, (hk * pairs + p) * LANES:(hk * pairs + p + 1) * LANES] for p in range(pairs)], axis=0)
        s = lax.dot_general(qg, kk, (((1,), (1,)), ((), ())), preferred_element_type=F32)
        rinv = []
        for p in range(pairs):
            rows = slice(p * w, (p + 1) * w)
            for e in range(2):
                cols = slice(e * 2 * w, (e + 1) * 2 * w)
                sink = sink_ref[hk * KV_GROUP + 2 * p + e]
                sc = jnp.where(no_prev, NEG_INF, s[rows, cols] + bias_ref[hk, rows, cols])
                m = jnp.maximum(jnp.max(sc, axis=-1, keepdims=True), sink)
                pe = jnp.exp(sc - m)
                denom = jnp.sum(pe, axis=-1, keepdims=True) + jnp.exp(sink - m)
                p_ref[hk, rows, cols] = pe.astype(BF16)
                rinv.append(1.0 / denom)
        o = jnp.dot(p_ref[hk], vv, preferred_element_type=F32)
        for p in range(pairs):
            scale = jnp.where(low1, rinv[2 * p], rinv[2 * p + 1])
            ocols = slice((hk * pairs + p) * LANES, (hk * pairs + p + 1) * LANES)
            o_ref[0, :, ocols] = (o[p * w:(p + 1) * w] * scale).astype(o_ref.dtype)


def _attention(q, kv2, sinks, bias, *, batch, seq):
    t, qd = q.shape
    kvw = kv2.shape[1]
    n_kv = kvw // (2 * LANES)
    pairs = KV_GROUP * HEAD_DIM // LANES
    nb = seq // WINDOW
    q3 = q.reshape(batch, seq, qd)
    kv3 = kv2.reshape(batch, seq, kvw)
    out = pl.pallas_call(
        functools.partial(_attn_kernel, n_kv=n_kv),
        grid=(batch, nb),
        in_specs=[
            pl.BlockSpec(memory_space=pltpu.SMEM),
            pl.BlockSpec((1, WINDOW, qd), lambda b, c: (b, c, 0)),
            pl.BlockSpec((1, WINDOW, kvw), lambda b, c: (b, jnp.maximum(c - 1, 0), 0)),
            pl.BlockSpec((1, WINDOW, kvw), lambda b, c: (b, c, 0)),
            pl.BlockSpec((n_kv, pairs * WINDOW, 4 * WINDOW), lambda b, c: (0, 0, 0)),
        ],
        out_specs=pl.BlockSpec((1, WINDOW, qd), lambda b, c: (b, c, 0)),
        out_shape=jax.ShapeDtypeStruct((batch, seq, qd), BF16),
        scratch_shapes=[pltpu.VMEM((n_kv, pairs * WINDOW, 4 * WINDOW), BF16)],
        compiler_params=_params(),
        name="swa_attention",
    )(sinks, q3, kv3, kv3, bias)
    return out.reshape(t, qd)


def _t5_bucket(dist):
    max_exact = N_BUCKETS // 2
    is_small = dist < max_exact
    d = jnp.maximum(dist, 1).astype(F32)
    large = max_exact + (jnp.log(d / max_exact) / math.log(MAX_DISTANCE / max_exact)
                         * (N_BUCKETS - max_exact)).astype(jnp.int32)
    large = jnp.minimum(large, N_BUCKETS - 1)
    return jnp.where(is_small, dist, large)


def _band_bias(rel_bias, n_kv):
    w = WINDOW
    n_heads = rel_bias.shape[1]
    by_dist = rel_bias[_t5_bucket(jnp.arange(w, dtype=jnp.int32))].astype(F32).T
    neg = jnp.full((n_heads, 1), NEG_INF, F32)
    r = jnp.concatenate([neg, by_dist[:, ::-1], jnp.broadcast_to(neg, (n_heads, w))], axis=1)
    band = jnp.tile(r, (1, w))[:, :w * 2 * w].reshape(n_heads, w, 2 * w)
    pairs = n_heads // n_kv // 2
    band = band.reshape(n_kv, pairs, 2, w, 2 * w).transpose(0, 1, 3, 2, 4)
    return band.reshape(n_kv, pairs * w, 4 * w)


def _duplicate_heads(a):
    lead = a.shape[:-1]
    n = a.shape[-1] // HEAD_DIM
    a = a.reshape(*lead, n, 1, HEAD_DIM)
    return jnp.broadcast_to(a, (*lead, n, LANES // HEAD_DIM, HEAD_DIM)).reshape(*lead, n * LANES)


def kernel(x, mix_norm, ffn_norm, a_w_in, a_norm_v, a_w_s, a_b_s, a_w_out, kv_norm, w_kv, b_kv, b_w_q, b_b_q, b_sinks, b_w_o, b_b_o, rel_bias, ffn_w_gate, ffn_w_up, ffn_w_down, final_norm):
    batch, seq, d = x.shape
    depth = mix_norm.shape[0]
    n_a = a_w_in.shape[0]
    h = x.reshape(batch * seq, d)
    n_kv = w_kv.shape[1] // (2 * HEAD_DIM)
    bias = _band_bias(rel_bias, n_kv)
    q_scale = 1.0 / math.sqrt(HEAD_DIM)
    kv = None
    for layer in range(depth):
        if layer < n_a:
            i = layer
            gated = _gmlp_in(h, mix_norm[layer], a_w_in, i, a_norm_v[i], a_w_s[i], a_b_s[i])
            h = _linear(gated, a_w_out, i, residual=h, out_dtype=F32, name="gmlp_out")
        else:
            i = layer - n_a
            q = _linear(h, b_w_q, i, gain=mix_norm[layer], bias=b_b_q[i], scale=q_scale,
                        out_dtype=BF16, name="q_proj")
            attn = _attention(q, kv, b_sinks[i], bias, batch=batch, seq=seq)
            h = _linear(attn, b_w_o, i, bias=b_b_o[i], residual=h, out_dtype=F32, name="o_proj")
        h = _ffn(h, ffn_norm[layer], ffn_w_gate, ffn_w_up, ffn_w_down, layer,
                 final_gain=final_norm if layer == depth - 1 else None)
        if layer == n_a - 1:
            kv = _linear(h, _duplicate_heads(w_kv)[None], 0, gain=kv_norm, bias=_duplicate_heads(b_kv),
                         out_dtype=BF16, name="kv_proj")
    return h.reshape(batch, seq, d)
```

```python
import functools
import math

import jax
import jax.numpy as jnp
import numpy as np
from jax import lax
from jax.experimental import pallas as pl
from jax.experimental.pallas import tpu as pltpu

F32 = jnp.float32
BF16 = jnp.bfloat16

RMS_EPS = 1e-5
NEG_INF = -1e30
LOG2E = math.log2(math.e)

CHUNK = 128
A_GROUPS = 8
HEAD_DIM = 64
KV_GROUP = 8
WINDOW = 128
N_BUCKETS = 32
MAX_DISTANCE = 128

V7X_VMEM_BYTES = 64 * 1024 * 1024
VMEM_LIMIT_BYTES = V7X_VMEM_BYTES - 8 * 1024 * 1024

TM_LINEAR = 1024
TN_LINEAR = 1024
TM_GMLP = 1024
TN_GMLP = 1024
GMLP_ROWS = 512
GMLP_COLS = 512
TM_FFN = 1024
TF_FFN = 256
FFN_DOWN_COLS = 512
NORM_ROWS = 256


def _params():
    return pltpu.CompilerParams(
        dimension_semantics=("arbitrary", "arbitrary"),
        vmem_limit_bytes=VMEM_LIMIT_BYTES,
    )


def _rmsnorm_rows(x_ref, g_ref, o_ref):
    rows = x_ref.shape[0]
    g = g_ref[...]

    def body(i, carry):
        r = pl.ds(pl.multiple_of(i * NORM_ROWS, NORM_ROWS), NORM_ROWS)
        x = x_ref[r, :]
        ms = jnp.mean(x * x, axis=-1, keepdims=True)
        o_ref[r, :] = ((x * lax.rsqrt(ms + RMS_EPS)) * g).astype(o_ref.dtype)
        return carry

    lax.fori_loop(0, rows // NORM_ROWS, body, 0)


def _linear_kernel(*refs, norm, bias, res, scale):
    refs = list(refs)
    x_ref = refs.pop(0)
    g_ref = refs.pop(0) if norm else None
    w_ref = refs.pop(0)
    b_ref = refs.pop(0) if bias else None
    r_ref = refs.pop(0) if res else None
    o_ref = refs.pop(0)
    wbf_ref = refs.pop(0)
    xn_ref = refs.pop(0) if norm else None
    m, j = pl.program_id(0), pl.program_id(1)

    @pl.when(m == 0)
    def _():
        wbf_ref[j] = w_ref[...].astype(BF16)

    if norm:
        @pl.when(j == 0)
        def _():
            _rmsnorm_rows(x_ref, g_ref, xn_ref)
        lhs = xn_ref[...]
    else:
        lhs = x_ref[...]
    acc = jnp.dot(lhs, wbf_ref[j], preferred_element_type=F32)
    if bias:
        acc = acc + b_ref[...]
    if scale != 1.0:
        acc = acc * scale
    if res:
        acc = acc + r_ref[...]
    o_ref[...] = acc.astype(o_ref.dtype)


def _linear(x, w, layer, *, gain=None, bias=None, residual=None, scale=1.0, out_dtype, name):
    t, k = x.shape
    n = w.shape[2]
    tm = min(TM_LINEAR, t)
    tn = min(TN_LINEAR, n)
    nj = n // tn
    norm = gain is not None
    in_specs = [pl.BlockSpec((tm, k), lambda m, j: (m, 0))]
    args = [x]
    if norm:
        in_specs.append(pl.BlockSpec((1, k), lambda m, j: (0, 0)))
        args.append(gain.reshape(1, k))
    in_specs.append(pl.BlockSpec((None, k, tn), lambda m, j: (layer, 0, jnp.where(m == 0, j, nj - 1))))
    args.append(w)
    if bias is not None:
        in_specs.append(pl.BlockSpec((1, tn), lambda m, j: (0, j)))
        args.append(bias.reshape(1, n))
    if residual is not None:
        in_specs.append(pl.BlockSpec((tm, tn), lambda m, j: (m, j)))
        args.append(residual)
    return pl.pallas_call(
        functools.partial(_linear_kernel, norm=norm, bias=bias is not None,
                          res=residual is not None, scale=scale),
        grid=(t // tm, nj),
        in_specs=in_specs,
        out_specs=pl.BlockSpec((tm, tn), lambda m, j: (m, j)),
        out_shape=jax.ShapeDtypeStruct((t, n), out_dtype),
        scratch_shapes=[pltpu.VMEM((nj, k, tn), BF16)] + ([pltpu.VMEM((tm, k), BF16)] if norm else []),
        compiler_params=_params(),
        name=name,
    )(*args)


def _gmlp_in_kernel(h_ref, g_ref, w_ref, nv_ref, ws_ref, bs_ref, o_ref,
                    xn_ref, vbuf_ref, ssq_ref, *, nj):
    j = pl.program_id(1)
    tm = h_ref.shape[0]
    tn = w_ref.shape[1]
    gw = nv_ref.shape[2]
    gpt = tn // gw

    @pl.when(j == 0)
    def _():
        _rmsnorm_rows(h_ref, g_ref, xn_ref)
        ssq_ref[...] = jnp.zeros_like(ssq_ref)

    def gelu_block(cb, r):
        w = w_ref[:, cb * GMLP_COLS:(cb + 1) * GMLP_COLS].astype(BF16)
        x = xn_ref[r * GMLP_ROWS:(r + 1) * GMLP_ROWS, :]
        return jax.nn.gelu(jnp.dot(x, w, preferred_element_type=F32))

    @pl.when(j < nj)
    def _():
        for r in range(tm // GMLP_ROWS):
            rows = slice(r * GMLP_ROWS, (r + 1) * GMLP_ROWS)
            ssq = ssq_ref[rows, :]
            for cb in range(tn // GMLP_COLS):
                v = gelu_block(cb, r)
                vbuf_ref[j, rows, cb * GMLP_COLS:(cb + 1) * GMLP_COLS] = v
                ssq = ssq + jnp.sum(v * v, axis=-1, keepdims=True)
            ssq_ref[rows, :] = ssq

    @pl.when(j >= nj)
    def _():
        ju = j - nj
        row = lax.broadcasted_iota(jnp.int32, (CHUNK, CHUNK), 0)
        col = lax.broadcasted_iota(jnp.int32, (CHUNK, CHUNK), 1)
        for r in range(tm // GMLP_ROWS):
            rows = slice(r * GMLP_ROWS, (r + 1) * GMLP_ROWS)
            rinv = lax.rsqrt(ssq_ref[rows, :] / (nj * tn) + RMS_EPS)
            for cb in range(tn // GMLP_COLS):
                u = gelu_block(cb, r)
                for gl in range(GMLP_COLS // gw):
                    g = ju * gpt + cb * (GMLP_COLS // gw) + gl
                    cols = slice(cb * GMLP_COLS + gl * gw, cb * GMLP_COLS + (gl + 1) * gw)
                    vn = ((vbuf_ref[ju, rows, cols] * rinv) * nv_ref[g]).astype(BF16)
                    wt = jnp.where(col <= row, ws_ref[g], 0.0).astype(BF16)
                    b = bs_ref[g]
                    for c in range(GMLP_ROWS // CHUNK):
                        crows = slice(c * CHUNK, (c + 1) * CHUNK)
                        s = jnp.dot(wt, vn[crows], preferred_element_type=F32) + b
                        o_ref[r * GMLP_ROWS + c * CHUNK:r * GMLP_ROWS + (c + 1) * CHUNK, cols] = (
                            u[crows, gl * gw:(gl + 1) * gw] * s).astype(o_ref.dtype)


def _gmlp_in(h, gain, w_in, layer, norm_v, w_s, b_s):
    t, d = h.shape
    half = w_in.shape[2] // 2
    groups = w_s.shape[0]
    gw = half // groups
    tm, tn = TM_GMLP, TN_GMLP
    nj = half // tn
    return pl.pallas_call(
        functools.partial(_gmlp_in_kernel, nj=nj),
        grid=(t // tm, 2 * nj),
        in_specs=[
            pl.BlockSpec((tm, d), lambda m, j: (m, 0)),
            pl.BlockSpec((1, d), lambda m, j: (0, 0)),
            pl.BlockSpec((None, d, tn), lambda m, j: (layer, 0, (j + nj) % (2 * nj))),
            pl.BlockSpec((groups, 1, gw), lambda m, j: (0, 0, 0)),
            pl.BlockSpec((groups, CHUNK, CHUNK), lambda m, j: (0, 0, 0)),
            pl.BlockSpec((groups, CHUNK, 1), lambda m, j: (0, 0, 0)),
        ],
        out_specs=pl.BlockSpec((tm, tn), lambda m, j: (m, jnp.maximum(j - nj, 0))),
        out_shape=jax.ShapeDtypeStruct((t, half), BF16),
        scratch_shapes=[
            pltpu.VMEM((tm, d), BF16),
            pltpu.VMEM((nj, tm, tn), F32),
            pltpu.VMEM((tm, 1), F32),
        ],
        compiler_params=_params(),
        name="gmlp_in",
    )(h, gain.reshape(1, d), w_in, norm_v.reshape(groups, 1, gw), w_s, b_s.reshape(groups, CHUNK, 1))


def _ffn_kernel(*refs, final):
    refs = list(refs)
    h_ref, g_ref, wg_ref, wu_ref, wd_ref = refs[:5]
    fg_ref = refs[5] if final else None
    o_ref, xn_ref = refs[-2:]
    f = pl.program_id(1)
    d = o_ref.shape[1]

    @pl.when(f == 0)
    def _():
        _rmsnorm_rows(h_ref, g_ref, xn_ref)
        o_ref[...] = h_ref[...]

    xn = xn_ref[...]
    gate = jnp.dot(xn, wg_ref[...].astype(BF16), preferred_element_type=F32)
    up = jnp.dot(xn, wu_ref[...].astype(BF16), preferred_element_type=F32)
    a = (jax.nn.silu(gate) * up).astype(BF16)
    for n in range(d // FFN_DOWN_COLS):
        cols = slice(n * FFN_DOWN_COLS, (n + 1) * FFN_DOWN_COLS)
        o_ref[:, cols] += jnp.dot(a, wd_ref[:, cols].astype(BF16), preferred_element_type=F32)

    if final:
        @pl.when(f == pl.num_programs(1) - 1)
        def _():
            _rmsnorm_rows(o_ref, fg_ref, o_ref)


def _ffn(h, gain, wg, wu, wd, layer, final_gain=None):
    t, d = h.shape
    ff = wg.shape[2]
    tm, tf = TM_FFN, TF_FFN
    final = final_gain is not None
    in_specs = [
        pl.BlockSpec((tm, d), lambda m, f: (m, 0)),
        pl.BlockSpec((1, d), lambda m, f: (0, 0)),
        pl.BlockSpec((None, d, tf), lambda m, f: (layer, 0, f)),
        pl.BlockSpec((None, d, tf), lambda m, f: (layer, 0, f)),
        pl.BlockSpec((None, tf, d), lambda m, f: (layer, f, 0)),
    ]
    args = [h, gain.reshape(1, d), wg, wu, wd]
    if final:
        in_specs.append(pl.BlockSpec((1, d), lambda m, f: (0, 0)))
        args.append(final_gain.reshape(1, d))
    return pl.pallas_call(
        functools.partial(_ffn_kernel, final=final),
        grid=(t // tm, ff // tf),
        in_specs=in_specs,
        out_specs=pl.BlockSpec((tm, d), lambda m, f: (m, 0)),
        out_shape=jax.ShapeDtypeStruct((t, d), F32),
        scratch_shapes=[pltpu.VMEM((tm, d), BF16)],
        compiler_params=_params(),
        name="ffn",
    )(*args)


LANES = 128


def _attn_kernel(sink_ref, q_ref, kvp_ref, kvc_ref, bias_ref, o_ref, p_ref, *, n_kv):
    w = WINDOW
    pairs = KV_GROUP * HEAD_DIM // LANES
    low2 = lax.broadcasted_iota(jnp.int32, (2 * w, LANES), 1) < HEAD_DIM
    low1 = lax.broadcasted_iota(jnp.int32, (w, LANES), 1) < HEAD_DIM
    zero = jnp.zeros((2 * w, LANES), BF16)

    def block_diag(prev, cur):
        x = jnp.concatenate([prev, cur], axis=0)
        return jnp.concatenate([jnp.where(low2, x, zero), jnp.where(low2, zero, x)], axis=0)

    for hk in range(n_kv):
        kcols = slice(hk * LANES, (hk + 1) * LANES)
        vcols = slice((n_kv + hk) * LANES, (n_kv + hk + 1) * LANES)
        kk = block_diag(kvp_ref[0, :, kcols], kvc_ref[0, :, kcols])
        vv = block_diag(kvp_ref[0, :, vcols], kvc_ref[0, :, vcols])
        qg = jnp.concatenate(
            [q_ref[0, :, (hk * pairs + p) * LANES:(hk * pairs + p + 1) * LANES] for p in range(pairs)], axis=0)
        s = lax.dot_general(qg, kk, (((1,), (1,)), ((), ())), preferred_element_type=F32)
        rinv = []
        for p in range(pairs):
            rows = slice(p * w, (p + 1) * w)
            for e in range(2):
                cols = slice(e * 2 * w, (e + 1) * 2 * w)
                sink = sink_ref[hk * KV_GROUP + 2 * p + e]
                sc = s[rows, cols] + bias_ref[hk, rows, cols]
                m = jnp.maximum(jnp.max(sc, axis=-1, keepdims=True), sink)
                pe = jnp.exp2(sc - m)
                denom = jnp.sum(pe, axis=-1, keepdims=True) + jnp.exp2(sink - m)
                p_ref[hk, rows, cols] = pe.astype(BF16)
                rinv.append(1.0 / denom)
        o = jnp.dot(p_ref[hk], vv, preferred_element_type=F32)
        for p in range(pairs):
            scale = jnp.where(low1, rinv[2 * p], rinv[2 * p + 1])
            ocols = slice((hk * pairs + p) * LANES, (hk * pairs + p + 1) * LANES)
            o_ref[0, :, ocols] = (o[p * w:(p + 1) * w] * scale).astype(o_ref.dtype)


def _attention(q, kv2, sinks, bias, *, batch, seq):
    t, qd = q.shape
    kvw = kv2.shape[1]
    n_kv = kvw // (2 * LANES)
    pairs = KV_GROUP * HEAD_DIM // LANES
    nb = seq // WINDOW
    q3 = q.reshape(batch, seq, qd)
    kv3 = kv2.reshape(batch, seq, kvw)
    out = pl.pallas_call(
        functools.partial(_attn_kernel, n_kv=n_kv),
        grid=(batch, nb),
        in_specs=[
            pl.BlockSpec(memory_space=pltpu.SMEM),
            pl.BlockSpec((1, WINDOW, qd), lambda b, c: (b, c, 0)),
            pl.BlockSpec((1, WINDOW, kvw), lambda b, c: (b, jnp.maximum(c - 1, 0), 0)),
            pl.BlockSpec((1, WINDOW, kvw), lambda b, c: (b, c, 0)),
            pl.BlockSpec((None, n_kv, pairs * WINDOW, 4 * WINDOW), lambda b, c: (jnp.minimum(c, 1), 0, 0, 0)),
        ],
        out_specs=pl.BlockSpec((1, WINDOW, qd), lambda b, c: (b, c, 0)),
        out_shape=jax.ShapeDtypeStruct((batch, seq, qd), BF16),
        scratch_shapes=[pltpu.VMEM((n_kv, pairs * WINDOW, 4 * WINDOW), BF16)],
        compiler_params=_params(),
        name="swa_attention",
    )(sinks, q3, kv3, kv3, bias)
    return out.reshape(t, qd)


def _t5_bucket(dist):
    max_exact = N_BUCKETS // 2
    is_small = dist < max_exact
    d = jnp.maximum(dist, 1).astype(F32)
    large = max_exact + (jnp.log(d / max_exact) / math.log(MAX_DISTANCE / max_exact)
                         * (N_BUCKETS - max_exact)).astype(jnp.int32)
    large = jnp.minimum(large, N_BUCKETS - 1)
    return jnp.where(is_small, dist, large)


def _band_bias(rel_bias, n_kv):
    w = WINDOW
    n_heads = rel_bias.shape[1]
    by_dist = rel_bias[_t5_bucket(jnp.arange(w, dtype=jnp.int32))].astype(F32).T
    neg = jnp.full((n_heads, 1), NEG_INF, F32)
    r = jnp.concatenate([neg, by_dist[:, ::-1], jnp.broadcast_to(neg, (n_heads, w))], axis=1)
    band = jnp.tile(r, (1, w))[:, :w * 2 * w].reshape(n_heads, w, 2 * w)
    pairs = n_heads // n_kv // 2
    band = band.reshape(n_kv, pairs, 2, w, 2 * w).transpose(0, 1, 3, 2, 4)
    first = band.at[..., :w].set(NEG_INF)
    return (jnp.stack([first, band]) * LOG2E).reshape(2, n_kv, pairs * w, 4 * w)


def _duplicate_heads(a):
    lead = a.shape[:-1]
    n = a.shape[-1] // HEAD_DIM
    a = a.reshape(*lead, n, 1, HEAD_DIM)
    return jnp.broadcast_to(a, (*lead, n, LANES // HEAD_DIM, HEAD_DIM)).reshape(*lead, n * LANES)


def kernel(x, mix_norm, ffn_norm, a_w_in, a_norm_v, a_w_s, a_b_s, a_w_out, kv_norm, w_kv, b_kv, b_w_q, b_b_q, b_sinks, b_w_o, b_b_o, rel_bias, ffn_w_gate, ffn_w_up, ffn_w_down, final_norm):
    batch, seq, d = x.shape
    depth = mix_norm.shape[0]
    n_a = a_w_in.shape[0]
    h = x.reshape(batch * seq, d)
    n_kv = w_kv.shape[1] // (2 * HEAD_DIM)
    bias = _band_bias(rel_bias, n_kv)
    q_scale = LOG2E / math.sqrt(HEAD_DIM)
    kv = None
    for layer in range(depth):
        if layer < n_a:
            i = layer
            gated = _gmlp_in(h, mix_norm[layer], a_w_in, i, a_norm_v[i], a_w_s[i], a_b_s[i])
            h = _linear(gated, a_w_out, i, residual=h, out_dtype=F32, name="gmlp_out")
        else:
            i = layer - n_a
            q = _linear(h, b_w_q, i, gain=mix_norm[layer], bias=b_b_q[i], scale=q_scale,
                        out_dtype=BF16, name="q_proj")
            attn = _attention(q, kv, b_sinks[i] * LOG2E, bias, batch=batch, seq=seq)
            h = _linear(attn, b_w_o, i, bias=b_b_o[i], residual=h, out_dtype=F32, name="o_proj")
        h = _ffn(h, ffn_norm[layer], ffn_w_gate, ffn_w_up, ffn_w_down, layer,
                 final_gain=final_norm if layer == depth - 1 else None)
        if layer == n_a - 1:
            kv = _linear(h, _duplicate_heads(w_kv)[None], 0, gain=kv_norm, bias=_duplicate_heads(b_kv),
                         out_dtype=BF16, name="kv_proj")
    return h.reshape(batch, seq, d)
```

```python
import functools
import math

import jax
import jax.numpy as jnp
from jax import lax
from jax.experimental import pallas as pl
from jax.experimental.pallas import tpu as pltpu

F32 = jnp.float32
BF16 = jnp.bfloat16

RMS_EPS = 1e-5
NEG_INF = -1e30
LOG2E = math.log2(math.e)

CHUNK = 128
A_GROUPS = 8
HEAD_DIM = 64
KV_GROUP = 8
WINDOW = 128
N_BUCKETS = 32
MAX_DISTANCE = 128

V7X_VMEM_BYTES = 64 * 1024 * 1024
VMEM_LIMIT_BYTES = V7X_VMEM_BYTES - 8 * 1024 * 1024

TM_LINEAR = 1024
TN_LINEAR = 1024
TM_GMLP = 1024
TN_GMLP = 1024
GMLP_ROWS = 512
GMLP_COLS = 512
TM_FFN = 1024
TF_FFN = 256
TF_FFN_BF16 = 512
FFN_DOWN_COLS = 512
NORM_ROWS = 256


def _params():
    return pltpu.CompilerParams(
        dimension_semantics=("arbitrary", "arbitrary"),
        vmem_limit_bytes=VMEM_LIMIT_BYTES,
    )


def _rmsnorm_rows(x_ref, g_ref, o_ref):
    rows = x_ref.shape[0]
    g = g_ref[...]

    def body(i, carry):
        r = pl.ds(pl.multiple_of(i * NORM_ROWS, NORM_ROWS), NORM_ROWS)
        x = x_ref[r, :]
        ms = jnp.mean(x * x, axis=-1, keepdims=True)
        o_ref[r, :] = ((x * lax.rsqrt(ms + RMS_EPS)) * g).astype(o_ref.dtype)
        return carry

    lax.fori_loop(0, rows // NORM_ROWS, body, 0)


def _linear_kernel(*refs, norm, bias, res, scale):
    refs = list(refs)
    x_ref = refs.pop(0)
    g_ref = refs.pop(0) if norm else None
    w_ref = refs.pop(0)
    b_ref = refs.pop(0) if bias else None
    r_ref = refs.pop(0) if res else None
    o_ref = refs.pop(0)
    wbf_ref = refs.pop(0)
    xn_ref = refs.pop(0) if norm else None
    m, j = pl.program_id(0), pl.program_id(1)

    @pl.when(m == 0)
    def _():
        wbf_ref[j] = w_ref[...].astype(BF16)

    if norm:
        @pl.when(j == 0)
        def _():
            _rmsnorm_rows(x_ref, g_ref, xn_ref)
        lhs = xn_ref[...]
    else:
        lhs = x_ref[...]
    acc = jnp.dot(lhs, wbf_ref[j], preferred_element_type=F32)
    if bias:
        acc = acc + b_ref[...]
    if scale != 1.0:
        acc = acc * scale
    if res:
        acc = acc + r_ref[...]
    o_ref[...] = acc.astype(o_ref.dtype)


def _linear(x, w, layer, *, gain=None, bias=None, residual=None, scale=1.0, out_dtype, name):
    t, k = x.shape
    n = w.shape[2]
    tm = min(TM_LINEAR, t)
    tn = min(TN_LINEAR, n)
    nj = n // tn
    norm = gain is not None
    in_specs = [pl.BlockSpec((tm, k), lambda m, j: (m, 0))]
    args = [x]
    if norm:
        in_specs.append(pl.BlockSpec((1, k), lambda m, j: (0, 0)))
        args.append(gain.reshape(1, k))
    in_specs.append(pl.BlockSpec((None, k, tn), lambda m, j: (layer, 0, jnp.where(m == 0, j, nj - 1))))
    args.append(w)
    if bias is not None:
        in_specs.append(pl.BlockSpec((1, tn), lambda m, j: (0, j)))
        args.append(bias.reshape(1, n))
    if residual is not None:
        in_specs.append(pl.BlockSpec((tm, tn), lambda m, j: (m, j)))
        args.append(residual)
    return pl.pallas_call(
        functools.partial(_linear_kernel, norm=norm, bias=bias is not None,
                          res=residual is not None, scale=scale),
        grid=(t // tm, nj),
        in_specs=in_specs,
        out_specs=pl.BlockSpec((tm, tn), lambda m, j: (m, j)),
        out_shape=jax.ShapeDtypeStruct((t, n), out_dtype),
        scratch_shapes=[pltpu.VMEM((nj, k, tn), BF16)] + ([pltpu.VMEM((tm, k), BF16)] if norm else []),
        compiler_params=_params(),
        name=name,
    )(*args)


def _gmlp_in_kernel(h_ref, g_ref, w_ref, nv_ref, ws_ref, bs_ref, o_ref,
                    xn_ref, vbuf_ref, ssq_ref, *, nj):
    j = pl.program_id(1)
    tm = h_ref.shape[0]
    tn = w_ref.shape[1]
    gw = nv_ref.shape[2]
    gpt = tn // gw

    @pl.when(j == 0)
    def _():
        _rmsnorm_rows(h_ref, g_ref, xn_ref)
        ssq_ref[...] = jnp.zeros_like(ssq_ref)

    def gelu_block(cb, r):
        w = w_ref[:, cb * GMLP_COLS:(cb + 1) * GMLP_COLS].astype(BF16)
        x = xn_ref[r * GMLP_ROWS:(r + 1) * GMLP_ROWS, :]
        return jax.nn.gelu(jnp.dot(x, w, preferred_element_type=F32))

    @pl.when(j < nj)
    def _():
        for r in range(tm // GMLP_ROWS):
            rows = slice(r * GMLP_ROWS, (r + 1) * GMLP_ROWS)
            ssq = ssq_ref[rows, :]
            for cb in range(tn // GMLP_COLS):
                v = gelu_block(cb, r)
                vbuf_ref[j, rows, cb * GMLP_COLS:(cb + 1) * GMLP_COLS] = v
                ssq = ssq + jnp.sum(v * v, axis=-1, keepdims=True)
            ssq_ref[rows, :] = ssq

    @pl.when(j >= nj)
    def _():
        ju = j - nj
        row = lax.broadcasted_iota(jnp.int32, (CHUNK, CHUNK), 0)
        col = lax.broadcasted_iota(jnp.int32, (CHUNK, CHUNK), 1)
        for r in range(tm // GMLP_ROWS):
            rows = slice(r * GMLP_ROWS, (r + 1) * GMLP_ROWS)
            rinv = lax.rsqrt(ssq_ref[rows, :] / (nj * tn) + RMS_EPS)
            for cb in range(tn // GMLP_COLS):
                u = gelu_block(cb, r)
                for gl in range(GMLP_COLS // gw):
                    g = ju * gpt + cb * (GMLP_COLS // gw) + gl
                    cols = slice(cb * GMLP_COLS + gl * gw, cb * GMLP_COLS + (gl + 1) * gw)
                    vn = ((vbuf_ref[ju, rows, cols] * rinv) * nv_ref[g]).astype(BF16)
                    wt = jnp.where(col <= row, ws_ref[g], 0.0).astype(BF16)
                    b = bs_ref[g]
                    for c in range(GMLP_ROWS // CHUNK):
                        crows = slice(c * CHUNK, (c + 1) * CHUNK)
                        s = jnp.dot(wt, vn[crows], preferred_element_type=F32) + b
                        o_ref[r * GMLP_ROWS + c * CHUNK:r * GMLP_ROWS + (c + 1) * CHUNK, cols] = (
                            u[crows, gl * gw:(gl + 1) * gw] * s).astype(o_ref.dtype)


def _gmlp_in(h, gain, w_in, layer, norm_v, w_s, b_s):
    t, d = h.shape
    half = w_in.shape[2] // 2
    groups = w_s.shape[0]
    gw = half // groups
    tm, tn = TM_GMLP, TN_GMLP
    nj = half // tn
    return pl.pallas_call(
        functools.partial(_gmlp_in_kernel, nj=nj),
        grid=(t // tm, 2 * nj),
        in_specs=[
            pl.BlockSpec((tm, d), lambda m, j: (m, 0)),
            pl.BlockSpec((1, d), lambda m, j: (0, 0)),
            pl.BlockSpec((None, d, tn), lambda m, j: (layer, 0, (j + nj) % (2 * nj))),
            pl.BlockSpec((groups, 1, gw), lambda m, j: (0, 0, 0)),
            pl.BlockSpec((groups, CHUNK, CHUNK), lambda m, j: (0, 0, 0)),
            pl.BlockSpec((groups, CHUNK, 1), lambda m, j: (0, 0, 0)),
        ],
        out_specs=pl.BlockSpec((tm, tn), lambda m, j: (m, jnp.maximum(j - nj, 0))),
        out_shape=jax.ShapeDtypeStruct((t, half), BF16),
        scratch_shapes=[
            pltpu.VMEM((tm, d), BF16),
            pltpu.VMEM((nj, tm, tn), F32),
            pltpu.VMEM((tm, 1), F32),
        ],
        compiler_params=_params(),
        name="gmlp_in",
    )(h, gain.reshape(1, d), w_in, norm_v.reshape(groups, 1, gw), w_s, b_s.reshape(groups, CHUNK, 1))


def _ffn_kernel(*refs, final, n_cast):
    refs = list(refs)
    h_ref, g_ref, wg_ref, wu_ref, wd_ref = refs[:5]
    fg_ref = refs[5] if final else None
    cast_in = refs[5 + final:5 + final + n_cast]
    o_ref = refs[5 + final + n_cast]
    cast_out = refs[6 + final + n_cast:6 + final + 2 * n_cast]
    xn_ref = refs[-1]
    f = pl.program_id(1)
    d = o_ref.shape[1]

    @pl.when(f == 0)
    def _():
        _rmsnorm_rows(h_ref, g_ref, xn_ref)
        o_ref[...] = h_ref[...]

    xn = xn_ref[...]
    gate = jnp.dot(xn, wg_ref[...].astype(BF16), preferred_element_type=F32)
    up = jnp.dot(xn, wu_ref[...].astype(BF16), preferred_element_type=F32)
    a = (jax.nn.silu(gate) * up).astype(BF16)
    for n in range(d // FFN_DOWN_COLS):
        cols = slice(n * FFN_DOWN_COLS, (n + 1) * FFN_DOWN_COLS)
        o_ref[:, cols] += jnp.dot(a, wd_ref[:, cols].astype(BF16), preferred_element_type=F32)

    for src, dst in zip(cast_in, cast_out):
        dst[...] = src[...].astype(BF16)

    if final:
        @pl.when(f == pl.num_programs(1) - 1)
        def _():
            _rmsnorm_rows(o_ref, fg_ref, o_ref)


def _ffn(h, gain, wg, wu, wd, layer=None, final_gain=None, cast_layer=None, cast_from=None):
    t, d = h.shape
    stacked = layer is not None
    ff = wg.shape[-1]
    tm = TM_FFN
    tf = TF_FFN if stacked else TF_FFN_BF16
    gm, gf = t // tm, ff // tf
    final = final_gain is not None
    if stacked:
        w_specs = [
            pl.BlockSpec((None, d, tf), lambda m, f: (layer, 0, f)),
            pl.BlockSpec((None, d, tf), lambda m, f: (layer, 0, f)),
            pl.BlockSpec((None, tf, d), lambda m, f: (layer, f, 0)),
        ]
    else:
        w_specs = [
            pl.BlockSpec((d, tf), lambda m, f: (0, f)),
            pl.BlockSpec((d, tf), lambda m, f: (0, f)),
            pl.BlockSpec((tf, d), lambda m, f: (f, 0)),
        ]
    in_specs = [pl.BlockSpec((tm, d), lambda m, f: (m, 0)), pl.BlockSpec((1, d), lambda m, f: (0, 0))] + w_specs
    args = [h, gain.reshape(1, d), wg, wu, wd]
    if final:
        in_specs.append(pl.BlockSpec((1, d), lambda m, f: (0, 0)))
        args.append(final_gain.reshape(1, d))
    out_specs = [pl.BlockSpec((tm, d), lambda m, f: (m, 0))]
    out_shape = [jax.ShapeDtypeStruct((t, d), F32)]
    n_cast = 0
    if cast_layer is not None:
        n_cast = 3
        rd, cf = d // gm, ff // gf
        in_specs += [
            pl.BlockSpec((None, rd, cf), lambda m, f: (cast_layer, m, f)),
            pl.BlockSpec((None, rd, cf), lambda m, f: (cast_layer, m, f)),
            pl.BlockSpec((None, cf, rd), lambda m, f: (cast_layer, f, m)),
        ]
        args += list(cast_from)
        out_specs += [
            pl.BlockSpec((rd, cf), lambda m, f: (m, f)),
            pl.BlockSpec((rd, cf), lambda m, f: (m, f)),
            pl.BlockSpec((cf, rd), lambda m, f: (f, m)),
        ]
        out_shape += [jax.ShapeDtypeStruct((d, ff), BF16), jax.ShapeDtypeStruct((d, ff), BF16),
                      jax.ShapeDtypeStruct((ff, d), BF16)]
    outs = pl.pallas_call(
        functools.partial(_ffn_kernel, final=final, n_cast=n_cast),
        grid=(gm, gf),
        in_specs=in_specs,
        out_specs=out_specs,
        out_shape=out_shape,
        scratch_shapes=[pltpu.VMEM((tm, d), BF16)],
        compiler_params=_params(),
        name="ffn",
    )(*args)
    return outs if n_cast else outs[0]


LANES = 128
BF16_SUBLANES = 16


def _attn_kernel(sink_ref, q_ref, kvp_ref, kvc_ref, bias_ref, *rest, n_kv, n_cast):
    cast_in, o_ref, cast_out, p_ref = rest[:n_cast], rest[n_cast], rest[n_cast + 1:2 * n_cast + 1], rest[-1]
    for src, dst in zip(cast_in, cast_out):
        dst[...] = src[...].astype(BF16)
    w = WINDOW
    pairs = KV_GROUP * HEAD_DIM // LANES
    low2 = lax.broadcasted_iota(jnp.int32, (2 * w, LANES), 1) < HEAD_DIM
    low1 = lax.broadcasted_iota(jnp.int32, (w, LANES), 1) < HEAD_DIM
    zero = jnp.zeros((2 * w, LANES), BF16)

    def block_diag(prev, cur):
        x = jnp.concatenate([prev, cur], axis=0)
        return jnp.concatenate([jnp.where(low2, x, zero), jnp.where(low2, zero, x)], axis=0)

    for hk in range(n_kv):
        kcols = slice(hk * LANES, (hk + 1) * LANES)
        vcols = slice((n_kv + hk) * LANES, (n_kv + hk + 1) * LANES)
        kk = block_diag(kvp_ref[0, :, kcols], kvc_ref[0, :, kcols])
        vv = block_diag(kvp_ref[0, :, vcols], kvc_ref[0, :, vcols])
        qg = jnp.concatenate(
            [q_ref[0, :, (hk * pairs + p) * LANES:(hk * pairs + p + 1) * LANES] for p in range(pairs)], axis=0)
        s = lax.dot_general(qg, kk, (((1,), (1,)), ((), ())), preferred_element_type=F32)
        rinv = []
        for p in range(pairs):
            rows = slice(p * w, (p + 1) * w)
            for e in range(2):
                cols = slice(e * 2 * w, (e + 1) * 2 * w)
                sink = sink_ref[hk * KV_GROUP + 2 * p + e]
                sc = s[rows, cols] + bias_ref[hk, rows, cols]
                m = jnp.maximum(jnp.max(sc, axis=-1, keepdims=True), sink)
                pe = jnp.exp2(sc - m)
                denom = jnp.sum(pe, axis=-1, keepdims=True) + jnp.exp2(sink - m)
                p_ref[hk, rows, cols] = pe.astype(BF16)
                rinv.append(1.0 / denom)
        o = jnp.dot(p_ref[hk], vv, preferred_element_type=F32)
        for p in range(pairs):
            scale = jnp.where(low1, rinv[2 * p], rinv[2 * p + 1])
            ocols = slice((hk * pairs + p) * LANES, (hk * pairs + p + 1) * LANES)
            o_ref[0, :, ocols] = (o[p * w:(p + 1) * w] * scale).astype(o_ref.dtype)


def _slab_rows(rows, steps):
    r = BF16_SUBLANES
    while rows % r or rows // r > steps:
        r += BF16_SUBLANES
    return r


def _attention(q, kv2, sinks, bias, *, batch, seq, cast_layer=None, cast_from=()):
    t, qd = q.shape
    kvw = kv2.shape[1]
    n_kv = kvw // (2 * LANES)
    pairs = KV_GROUP * HEAD_DIM // LANES
    nb = seq // WINDOW
    q3 = q.reshape(batch, seq, qd)
    kv3 = kv2.reshape(batch, seq, kvw)
    in_specs = [
        pl.BlockSpec(memory_space=pltpu.SMEM),
        pl.BlockSpec((1, WINDOW, qd), lambda b, c: (b, c, 0)),
        pl.BlockSpec((1, WINDOW, kvw), lambda b, c: (b, jnp.maximum(c - 1, 0), 0)),
        pl.BlockSpec((1, WINDOW, kvw), lambda b, c: (b, c, 0)),
        pl.BlockSpec((None, n_kv, pairs * WINDOW, 4 * WINDOW), lambda b, c: (jnp.minimum(c, 1), 0, 0, 0)),
    ]
    out_specs = [pl.BlockSpec((1, WINDOW, qd), lambda b, c: (b, c, 0))]
    out_shape = [jax.ShapeDtypeStruct((batch, seq, qd), BF16)]
    for w3 in cast_from:
        rows, cols = w3.shape[1:]
        r = _slab_rows(rows, batch * nb)
        last = rows // r - 1
        in_specs.append(pl.BlockSpec(
            (None, r, cols), lambda b, c, last=last: (cast_layer, jnp.minimum(b * nb + c, last), 0)))
        out_specs.append(pl.BlockSpec((r, cols), lambda b, c, last=last: (jnp.minimum(b * nb + c, last), 0)))
        out_shape.append(jax.ShapeDtypeStruct((rows, cols), BF16))
    outs = pl.pallas_call(
        functools.partial(_attn_kernel, n_kv=n_kv, n_cast=len(cast_from)),
        grid=(batch, nb),
        in_specs=in_specs,
        out_specs=out_specs,
        out_shape=out_shape,
        scratch_shapes=[pltpu.VMEM((n_kv, pairs * WINDOW, 4 * WINDOW), BF16)],
        compiler_params=_params(),
        name="swa_attention",
    )(sinks, q3, kv3, kv3, bias, *cast_from)
    return (outs[0].reshape(t, qd), *outs[1:])


def _t5_bucket(dist):
    max_exact = N_BUCKETS // 2
    is_small = dist < max_exact
    d = jnp.maximum(dist, 1).astype(F32)
    large = max_exact + (jnp.log(d / max_exact) / math.log(MAX_DISTANCE / max_exact)
                         * (N_BUCKETS - max_exact)).astype(jnp.int32)
    large = jnp.minimum(large, N_BUCKETS - 1)
    return jnp.where(is_small, dist, large)


def _band_bias(rel_bias, n_kv):
    w = WINDOW
    n_heads = rel_bias.shape[1]
    by_dist = rel_bias[_t5_bucket(jnp.arange(w, dtype=jnp.int32))].astype(F32).T
    neg = jnp.full((n_heads, 1), NEG_INF, F32)
    r = jnp.concatenate([neg, by_dist[:, ::-1], jnp.broadcast_to(neg, (n_heads, w))], axis=1)
    band = jnp.tile(r, (1, w))[:, :w * 2 * w].reshape(n_heads, w, 2 * w)
    pairs = n_heads // n_kv // 2
    band = band.reshape(n_kv, pairs, 2, w, 2 * w).transpose(0, 1, 3, 2, 4)
    first = band.at[..., :w].set(NEG_INF)
    return (jnp.stack([first, band]) * LOG2E).reshape(2, n_kv, pairs * w, 4 * w)


def _duplicate_heads(a):
    lead = a.shape[:-1]
    n = a.shape[-1] // HEAD_DIM
    a = a.reshape(*lead, n, 1, HEAD_DIM)
    return jnp.broadcast_to(a, (*lead, n, LANES // HEAD_DIM, HEAD_DIM)).reshape(*lead, n * LANES)


def kernel(x, mix_norm, ffn_norm, a_w_in, a_norm_v, a_w_s, a_b_s, a_w_out, kv_norm, w_kv, b_kv, b_w_q, b_b_q, b_sinks, b_w_o, b_b_o, rel_bias, ffn_w_gate, ffn_w_up, ffn_w_down, final_norm):
    batch, seq, d = x.shape
    depth = mix_norm.shape[0]
    n_a = a_w_in.shape[0]
    h = x.reshape(batch * seq, d)
    n_kv = w_kv.shape[1] // (2 * HEAD_DIM)
    bias = _band_bias(rel_bias, n_kv)
    q_scale = LOG2E / math.sqrt(HEAD_DIM)
    kv = None
    ffn_stacks = (ffn_w_gate, ffn_w_up, ffn_w_down)
    ffn_bf16 = None
    for layer in range(depth):
        if layer < n_a:
            i = layer
            gated = _gmlp_in(h, mix_norm[layer], a_w_in, i, a_norm_v[i], a_w_s[i], a_b_s[i])
            h = _linear(gated, a_w_out, i, residual=h, out_dtype=F32, name="gmlp_out")
        else:
            i = layer - n_a
            q = _linear(h, b_w_q, i, gain=mix_norm[layer], bias=b_b_q[i], scale=q_scale,
                        out_dtype=BF16, name="q_proj")
            attn, *ffn_bf16 = _attention(q, kv, b_sinks[i] * LOG2E, bias, batch=batch, seq=seq,
                                         cast_layer=layer, cast_from=ffn_stacks)
            h = _linear(attn, b_w_o, i, bias=b_b_o[i], residual=h, out_dtype=F32, name="o_proj")
        final_gain = final_norm if layer == depth - 1 else None
        if ffn_bf16:
            h = _ffn(h, ffn_norm[layer], *ffn_bf16, final_gain=final_gain)
            ffn_bf16 = None
        elif layer + 1 < n_a:
            h, *ffn_bf16 = _ffn(h, ffn_norm[layer], *ffn_stacks, layer=layer, final_gain=final_gain,
                                cast_layer=layer + 1, cast_from=ffn_stacks)
        else:
            h = _ffn(h, ffn_norm[layer], *ffn_stacks, layer=layer, final_gain=final_gain)
        if layer == n_a - 1:
            kv = _linear(h, _duplicate_heads(w_kv)[None], 0, gain=kv_norm, bias=_duplicate_heads(b_kv),
                         out_dtype=BF16, name="kv_proj")
    return h.reshape(batch, seq, d)
```

```python
import functools
import math

import jax
import jax.numpy as jnp
from jax import lax
from jax.experimental import pallas as pl
from jax.experimental.pallas import tpu as pltpu

F32 = jnp.float32
BF16 = jnp.bfloat16

RMS_EPS = 1e-5
NEG_INF = -1e30
LOG2E = math.log2(math.e)

CHUNK = 128
A_GROUPS = 8
HEAD_DIM = 64
KV_GROUP = 8
WINDOW = 128
N_BUCKETS = 32
MAX_DISTANCE = 128

V7X_VMEM_BYTES = 64 * 1024 * 1024
VMEM_LIMIT_BYTES = V7X_VMEM_BYTES - 8 * 1024 * 1024

TM_LINEAR = 1024
TN_LINEAR = 1024
TM_GMLP = 1024
TN_GMLP = 1024
GMLP_ROWS = 512
GMLP_COLS = 512
TM_FFN = 1024
TF_FFN = 256
TF_FFN_BF16 = 512
FFN_DOWN_COLS = 512
NORM_ROWS = 256
FFN_CAST_STEPS_PER_SLAB = 2
ATTN_CAST_STEPS_PER_SLAB = 4


def _params():
    return pltpu.CompilerParams(
        dimension_semantics=("arbitrary", "arbitrary"),
        vmem_limit_bytes=VMEM_LIMIT_BYTES,
    )


def _rmsnorm_rows(x_ref, g_ref, o_ref):
    rows = x_ref.shape[0]
    g = g_ref[...]

    def body(i, carry):
        r = pl.ds(pl.multiple_of(i * NORM_ROWS, NORM_ROWS), NORM_ROWS)
        x = x_ref[r, :]
        ms = jnp.mean(x * x, axis=-1, keepdims=True)
        o_ref[r, :] = ((x * lax.rsqrt(ms + RMS_EPS)) * g).astype(o_ref.dtype)
        return carry

    lax.fori_loop(0, rows // NORM_ROWS, body, 0)


BF16_SUBLANES = 16


def _piece_rows(rows, steps):
    r = BF16_SUBLANES
    while rows % r or rows // r > steps:
        r += BF16_SUBLANES
    return r


def _cast_specs(w3, layer, n_steps, steps_per_slab, step_of):
    rows, cols = w3.shape[1:]
    piece = _piece_rows(rows, n_steps)
    slab = piece * steps_per_slab
    last = rows // slab - 1

    def slab_index(*ids):
        return jnp.minimum(step_of(*ids) // steps_per_slab, last)

    in_spec = pl.BlockSpec((None, slab, cols), lambda *ids: (layer, slab_index(*ids), 0))
    out_spec = pl.BlockSpec((slab, cols), lambda *ids: (slab_index(*ids), 0))
    return in_spec, out_spec, jax.ShapeDtypeStruct((rows, cols), BF16)


def _cast_piece(step, src_ref, dst_ref, steps_per_slab):
    piece = src_ref.shape[0] // steps_per_slab
    r = pl.ds(pl.multiple_of((step % steps_per_slab) * piece, piece), piece)
    dst_ref[r, :] = src_ref[r, :].astype(BF16)


def _linear_kernel(*refs, norm, bias, res, scale):
    refs = list(refs)
    x_ref = refs.pop(0)
    g_ref = refs.pop(0) if norm else None
    w_ref = refs.pop(0)
    b_ref = refs.pop(0) if bias else None
    r_ref = refs.pop(0) if res else None
    o_ref = refs.pop(0)
    wbf_ref = refs.pop(0)
    xn_ref = refs.pop(0) if norm else None
    m, j = pl.program_id(0), pl.program_id(1)

    @pl.when(m == 0)
    def _():
        wbf_ref[j] = w_ref[...].astype(BF16)

    if norm:
        @pl.when(j == 0)
        def _():
            _rmsnorm_rows(x_ref, g_ref, xn_ref)
        lhs = xn_ref[...]
    else:
        lhs = x_ref[...]
    acc = jnp.dot(lhs, wbf_ref[j], preferred_element_type=F32)
    if bias:
        acc = acc + b_ref[...]
    if scale != 1.0:
        acc = acc * scale
    if res:
        acc = acc + r_ref[...]
    o_ref[...] = acc.astype(o_ref.dtype)


def _linear(x, w, layer, *, gain=None, bias=None, residual=None, scale=1.0, out_dtype, name):
    t, k = x.shape
    n = w.shape[2]
    tm = min(TM_LINEAR, t)
    tn = min(TN_LINEAR, n)
    nj = n // tn
    norm = gain is not None
    in_specs = [pl.BlockSpec((tm, k), lambda m, j: (m, 0))]
    args = [x]
    if norm:
        in_specs.append(pl.BlockSpec((1, k), lambda m, j: (0, 0)))
        args.append(gain.reshape(1, k))
    in_specs.append(pl.BlockSpec((None, k, tn), lambda m, j: (layer, 0, jnp.where(m == 0, j, nj - 1))))
    args.append(w)
    if bias is not None:
        in_specs.append(pl.BlockSpec((1, tn), lambda m, j: (0, j)))
        args.append(bias.reshape(1, n))
    if residual is not None:
        in_specs.append(pl.BlockSpec((tm, tn), lambda m, j: (m, j)))
        args.append(residual)
    return pl.pallas_call(
        functools.partial(_linear_kernel, norm=norm, bias=bias is not None,
                          res=residual is not None, scale=scale),
        grid=(t // tm, nj),
        in_specs=in_specs,
        out_specs=pl.BlockSpec((tm, tn), lambda m, j: (m, j)),
        out_shape=jax.ShapeDtypeStruct((t, n), out_dtype),
        scratch_shapes=[pltpu.VMEM((nj, k, tn), BF16)] + ([pltpu.VMEM((tm, k), BF16)] if norm else []),
        compiler_params=_params(),
        name=name,
    )(*args)


def _gmlp_in_kernel(h_ref, g_ref, w_ref, nv_ref, ws_ref, bs_ref, o_ref,
                    xn_ref, vbuf_ref, ssq_ref, *, nj):
    j = pl.program_id(1)
    tm = h_ref.shape[0]
    tn = w_ref.shape[1]
    gw = nv_ref.shape[2]
    gpt = tn // gw

    @pl.when(j == 0)
    def _():
        _rmsnorm_rows(h_ref, g_ref, xn_ref)
        ssq_ref[...] = jnp.zeros_like(ssq_ref)

    def gelu_block(cb, r):
        w = w_ref[:, cb * GMLP_COLS:(cb + 1) * GMLP_COLS].astype(BF16)
        x = xn_ref[r * GMLP_ROWS:(r + 1) * GMLP_ROWS, :]
        return jax.nn.gelu(jnp.dot(x, w, preferred_element_type=F32))

    @pl.when(j < nj)
    def _():
        for r in range(tm // GMLP_ROWS):
            rows = slice(r * GMLP_ROWS, (r + 1) * GMLP_ROWS)
            ssq = ssq_ref[rows, :]
            for cb in range(tn // GMLP_COLS):
                v = gelu_block(cb, r)
                vbuf_ref[j, rows, cb * GMLP_COLS:(cb + 1) * GMLP_COLS] = v
                ssq = ssq + jnp.sum(v * v, axis=-1, keepdims=True)
            ssq_ref[rows, :] = ssq

    @pl.when(j >= nj)
    def _():
        ju = j - nj
        row = lax.broadcasted_iota(jnp.int32, (CHUNK, CHUNK), 0)
        col = lax.broadcasted_iota(jnp.int32, (CHUNK, CHUNK), 1)
        for r in range(tm // GMLP_ROWS):
            rows = slice(r * GMLP_ROWS, (r + 1) * GMLP_ROWS)
            rinv = lax.rsqrt(ssq_ref[rows, :] / (nj * tn) + RMS_EPS)
            for cb in range(tn // GMLP_COLS):
                u = gelu_block(cb, r)
                for gl in range(GMLP_COLS // gw):
                    g = ju * gpt + cb * (GMLP_COLS // gw) + gl
                    cols = slice(cb * GMLP_COLS + gl * gw, cb * GMLP_COLS + (gl + 1) * gw)
                    vn = ((vbuf_ref[ju, rows, cols] * rinv) * nv_ref[g]).astype(BF16)
                    wt = jnp.where(col <= row, ws_ref[g], 0.0).astype(BF16)
                    b = bs_ref[g]
                    for c in range(GMLP_ROWS // CHUNK):
                        crows = slice(c * CHUNK, (c + 1) * CHUNK)
                        s = jnp.dot(wt, vn[crows], preferred_element_type=F32) + b
                        o_ref[r * GMLP_ROWS + c * CHUNK:r * GMLP_ROWS + (c + 1) * CHUNK, cols] = (
                            u[crows, gl * gw:(gl + 1) * gw] * s).astype(o_ref.dtype)


def _gmlp_in(h, gain, w_in, layer, norm_v, w_s, b_s):
    t, d = h.shape
    half = w_in.shape[2] // 2
    groups = w_s.shape[0]
    gw = half // groups
    tm, tn = TM_GMLP, TN_GMLP
    nj = half // tn
    return pl.pallas_call(
        functools.partial(_gmlp_in_kernel, nj=nj),
        grid=(t // tm, 2 * nj),
        in_specs=[
            pl.BlockSpec((tm, d), lambda m, j: (m, 0)),
            pl.BlockSpec((1, d), lambda m, j: (0, 0)),
            pl.BlockSpec((None, d, tn), lambda m, j: (layer, 0, (j + nj) % (2 * nj))),
            pl.BlockSpec((groups, 1, gw), lambda m, j: (0, 0, 0)),
            pl.BlockSpec((groups, CHUNK, CHUNK), lambda m, j: (0, 0, 0)),
            pl.BlockSpec((groups, CHUNK, 1), lambda m, j: (0, 0, 0)),
        ],
        out_specs=pl.BlockSpec((tm, tn), lambda m, j: (m, jnp.maximum(j - nj, 0))),
        out_shape=jax.ShapeDtypeStruct((t, half), BF16),
        scratch_shapes=[
            pltpu.VMEM((tm, d), BF16),
            pltpu.VMEM((nj, tm, tn), F32),
            pltpu.VMEM((tm, 1), F32),
        ],
        compiler_params=_params(),
        name="gmlp_in",
    )(h, gain.reshape(1, d), w_in, norm_v.reshape(groups, 1, gw), w_s, b_s.reshape(groups, CHUNK, 1))


def _ffn_kernel(*refs, final, n_cast):
    refs = list(refs)
    h_ref, g_ref, wg_ref, wu_ref, wd_ref = refs[:5]
    fg_ref = refs[5] if final else None
    cast_in = refs[5 + final:5 + final + n_cast]
    o_ref = refs[5 + final + n_cast]
    cast_out = refs[6 + final + n_cast:6 + final + 2 * n_cast]
    xn_ref = refs[-1]
    f = pl.program_id(1)
    d = o_ref.shape[1]

    @pl.when(f == 0)
    def _():
        _rmsnorm_rows(h_ref, g_ref, xn_ref)
        o_ref[...] = h_ref[...]

    xn = xn_ref[...]
    gate = jnp.dot(xn, wg_ref[...].astype(BF16), preferred_element_type=F32)
    up = jnp.dot(xn, wu_ref[...].astype(BF16), preferred_element_type=F32)
    a = (jax.nn.silu(gate) * up).astype(BF16)
    for n in range(d // FFN_DOWN_COLS):
        cols = slice(n * FFN_DOWN_COLS, (n + 1) * FFN_DOWN_COLS)
        o_ref[:, cols] += jnp.dot(a, wd_ref[:, cols].astype(BF16), preferred_element_type=F32)

    step = pl.program_id(0) * pl.num_programs(1) + f
    for src, dst in zip(cast_in, cast_out):
        _cast_piece(step, src, dst, FFN_CAST_STEPS_PER_SLAB)

    if final:
        @pl.when(f == pl.num_programs(1) - 1)
        def _():
            _rmsnorm_rows(o_ref, fg_ref, o_ref)


def _ffn(h, gain, wg, wu, wd, layer=None, final_gain=None, cast_layer=None, cast_from=None):
    t, d = h.shape
    stacked = layer is not None
    ff = wg.shape[-1]
    tm = TM_FFN
    tf = TF_FFN if stacked else TF_FFN_BF16
    gm, gf = t // tm, ff // tf
    final = final_gain is not None
    if stacked:
        w_specs = [
            pl.BlockSpec((None, d, tf), lambda m, f: (layer, 0, f)),
            pl.BlockSpec((None, d, tf), lambda m, f: (layer, 0, f)),
            pl.BlockSpec((None, tf, d), lambda m, f: (layer, f, 0)),
        ]
    else:
        w_specs = [
            pl.BlockSpec((d, tf), lambda m, f: (0, f)),
            pl.BlockSpec((d, tf), lambda m, f: (0, f)),
            pl.BlockSpec((tf, d), lambda m, f: (f, 0)),
        ]
    in_specs = [pl.BlockSpec((tm, d), lambda m, f: (m, 0)), pl.BlockSpec((1, d), lambda m, f: (0, 0))] + w_specs
    args = [h, gain.reshape(1, d), wg, wu, wd]
    if final:
        in_specs.append(pl.BlockSpec((1, d), lambda m, f: (0, 0)))
        args.append(final_gain.reshape(1, d))
    out_specs = [pl.BlockSpec((tm, d), lambda m, f: (m, 0))]
    out_shape = [jax.ShapeDtypeStruct((t, d), F32)]
    n_cast = 0
    if cast_layer is not None:
        n_cast = len(cast_from)
        for w3 in cast_from:
            i_spec, o_spec, o_shape = _cast_specs(
                w3, cast_layer, gm * gf, FFN_CAST_STEPS_PER_SLAB, lambda m, f: m * gf + f)
            in_specs.append(i_spec)
            out_specs.append(o_spec)
            out_shape.append(o_shape)
        args += list(cast_from)
    outs = pl.pallas_call(
        functools.partial(_ffn_kernel, final=final, n_cast=n_cast),
        grid=(gm, gf),
        in_specs=in_specs,
        out_specs=out_specs,
        out_shape=out_shape,
        scratch_shapes=[pltpu.VMEM((tm, d), BF16)],
        compiler_params=_params(),
        name="ffn",
    )(*args)
    return outs if n_cast else outs[0]


LANES = 128


def _attn_kernel(sink_ref, q_ref, kvp_ref, kvc_ref, bias_ref, *rest, n_kv, n_cast):
    cast_in, o_ref, cast_out, p_ref = rest[:n_cast], rest[n_cast], rest[n_cast + 1:2 * n_cast + 1], rest[-1]
    step = pl.program_id(0) * pl.num_programs(1) + pl.program_id(1)
    for src, dst in zip(cast_in, cast_out):
        _cast_piece(step, src, dst, ATTN_CAST_STEPS_PER_SLAB)
    w = WINDOW
    pairs = KV_GROUP * HEAD_DIM // LANES
    low2 = lax.broadcasted_iota(jnp.int32, (2 * w, LANES), 1) < HEAD_DIM
    low1 = lax.broadcasted_iota(jnp.int32, (w, LANES), 1) < HEAD_DIM
    zero = jnp.zeros((2 * w, LANES), BF16)

    def block_diag(prev, cur):
        x = jnp.concatenate([prev, cur], axis=0)
        return jnp.concatenate([jnp.where(low2, x, zero), jnp.where(low2, zero, x)], axis=0)

    for hk in range(n_kv):
        kcols = slice(hk * LANES, (hk + 1) * LANES)
        vcols = slice((n_kv + hk) * LANES, (n_kv + hk + 1) * LANES)
        kk = block_diag(kvp_ref[0, :, kcols], kvc_ref[0, :, kcols])
        vv = block_diag(kvp_ref[0, :, vcols], kvc_ref[0, :, vcols])
        qg = jnp.concatenate(
            [q_ref[0, :, (hk * pairs + p) * LANES:(hk * pairs + p + 1) * LANES] for p in range(pairs)], axis=0)
        s = lax.dot_general(qg, kk, (((1,), (1,)), ((), ())), preferred_element_type=F32)
        rinv = []
        for p in range(pairs):
            rows = slice(p * w, (p + 1) * w)
            for e in range(2):
                cols = slice(e * 2 * w, (e + 1) * 2 * w)
                sink = sink_ref[hk * KV_GROUP + 2 * p + e]
                sc = s[rows, cols] + bias_ref[hk, rows, cols]
                m = jnp.maximum(jnp.max(sc, axis=-1, keepdims=True), sink)
                pe = jnp.exp2(sc - m)
                denom = jnp.sum(pe, axis=-1, keepdims=True) + jnp.exp2(sink - m)
                p_ref[hk, rows, cols] = pe.astype(BF16)
                rinv.append(1.0 / denom)
        o = jnp.dot(p_ref[hk], vv, preferred_element_type=F32)
        for p in range(pairs):
            scale = jnp.where(low1, rinv[2 * p], rinv[2 * p + 1])
            ocols = slice((hk * pairs + p) * LANES, (hk * pairs + p + 1) * LANES)
            o_ref[0, :, ocols] = (o[p * w:(p + 1) * w] * scale).astype(o_ref.dtype)


def _attention(q, kv2, sinks, bias, *, batch, seq, cast_layer=None, cast_from=()):
    t, qd = q.shape
    kvw = kv2.shape[1]
    n_kv = kvw // (2 * LANES)
    pairs = KV_GROUP * HEAD_DIM // LANES
    nb = seq // WINDOW
    q3 = q.reshape(batch, seq, qd)
    kv3 = kv2.reshape(batch, seq, kvw)
    in_specs = [
        pl.BlockSpec(memory_space=pltpu.SMEM),
        pl.BlockSpec((1, WINDOW, qd), lambda b, c: (b, c, 0)),
        pl.BlockSpec((1, WINDOW, kvw), lambda b, c: (b, jnp.maximum(c - 1, 0), 0)),
        pl.BlockSpec((1, WINDOW, kvw), lambda b, c: (b, c, 0)),
        pl.BlockSpec((None, n_kv, pairs * WINDOW, 4 * WINDOW), lambda b, c: (jnp.minimum(c, 1), 0, 0, 0)),
    ]
    out_specs = [pl.BlockSpec((1, WINDOW, qd), lambda b, c: (b, c, 0))]
    out_shape = [jax.ShapeDtypeStruct((batch, seq, qd), BF16)]
    for w3 in cast_from:
        i_spec, o_spec, o_shape = _cast_specs(
            w3, cast_layer, batch * nb, ATTN_CAST_STEPS_PER_SLAB, lambda b, c: b * nb + c)
        in_specs.append(i_spec)
        out_specs.append(o_spec)
        out_shape.append(o_shape)
    outs = pl.pallas_call(
        functools.partial(_attn_kernel, n_kv=n_kv, n_cast=len(cast_from)),
        grid=(batch, nb),
        in_specs=in_specs,
        out_specs=out_specs,
        out_shape=out_shape,
        scratch_shapes=[pltpu.VMEM((n_kv, pairs * WINDOW, 4 * WINDOW), BF16)],
        compiler_params=_params(),
        name="swa_attention",
    )(sinks, q3, kv3, kv3, bias, *cast_from)
    return (outs[0].reshape(t, qd), *outs[1:])


def _t5_bucket(dist):
    max_exact = N_BUCKETS // 2
    is_small = dist < max_exact
    d = jnp.maximum(dist, 1).astype(F32)
    large = max_exact + (jnp.log(d / max_exact) / math.log(MAX_DISTANCE / max_exact)
                         * (N_BUCKETS - max_exact)).astype(jnp.int32)
    large = jnp.minimum(large, N_BUCKETS - 1)
    return jnp.where(is_small, dist, large)


def _band_bias(rel_bias, n_kv):
    w = WINDOW
    n_heads = rel_bias.shape[1]
    by_dist = rel_bias[_t5_bucket(jnp.arange(w, dtype=jnp.int32))].astype(F32).T
    neg = jnp.full((n_heads, 1), NEG_INF, F32)
    r = jnp.concatenate([neg, by_dist[:, ::-1], jnp.broadcast_to(neg, (n_heads, w))], axis=1)
    band = jnp.tile(r, (1, w))[:, :w * 2 * w].reshape(n_heads, w, 2 * w)
    pairs = n_heads // n_kv // 2
    band = band.reshape(n_kv, pairs, 2, w, 2 * w).transpose(0, 1, 3, 2, 4)
    first = band.at[..., :w].set(NEG_INF)
    return (jnp.stack([first, band]) * LOG2E).reshape(2, n_kv, pairs * w, 4 * w)


def _duplicate_heads(a):
    lead = a.shape[:-1]
    n = a.shape[-1] // HEAD_DIM
    a = a.reshape(*lead, n, 1, HEAD_DIM)
    return jnp.broadcast_to(a, (*lead, n, LANES // HEAD_DIM, HEAD_DIM)).reshape(*lead, n * LANES)


def kernel(x, mix_norm, ffn_norm, a_w_in, a_norm_v, a_w_s, a_b_s, a_w_out, kv_norm, w_kv, b_kv, b_w_q, b_b_q, b_sinks, b_w_o, b_b_o, rel_bias, ffn_w_gate, ffn_w_up, ffn_w_down, final_norm):
    batch, seq, d = x.shape
    depth = mix_norm.shape[0]
    n_a = a_w_in.shape[0]
    h = x.reshape(batch * seq, d)
    n_kv = w_kv.shape[1] // (2 * HEAD_DIM)
    bias = _band_bias(rel_bias, n_kv)
    q_scale = LOG2E / math.sqrt(HEAD_DIM)
    kv = None
    ffn_stacks = (ffn_w_gate, ffn_w_up, ffn_w_down)
    ffn_bf16 = None
    for layer in range(depth):
        if layer < n_a:
            i = layer
            gated = _gmlp_in(h, mix_norm[layer], a_w_in, i, a_norm_v[i], a_w_s[i], a_b_s[i])
            h = _linear(gated, a_w_out, i, residual=h, out_dtype=F32, name="gmlp_out")
        else:
            i = layer - n_a
            q = _linear(h, b_w_q, i, gain=mix_norm[layer], bias=b_b_q[i], scale=q_scale,
                        out_dtype=BF16, name="q_proj")
            attn, *ffn_bf16 = _attention(q, kv, b_sinks[i] * LOG2E, bias, batch=batch, seq=seq,
                                         cast_layer=layer, cast_from=ffn_stacks)
            h = _linear(attn, b_w_o, i, bias=b_b_o[i], residual=h, out_dtype=F32, name="o_proj")
        final_gain = final_norm if layer == depth - 1 else None
        if ffn_bf16:
            h = _ffn(h, ffn_norm[layer], *ffn_bf16, final_gain=final_gain)
            ffn_bf16 = None
        elif layer + 1 < n_a:
            h, *ffn_bf16 = _ffn(h, ffn_norm[layer], *ffn_stacks, layer=layer, final_gain=final_gain,
                                cast_layer=layer + 1, cast_from=ffn_stacks)
        else:
            h = _ffn(h, ffn_norm[layer], *ffn_stacks, layer=layer, final_gain=final_gain)
        if layer == n_a - 1:
            kv = _linear(h, _duplicate_heads(w_kv)[None], 0, gain=kv_norm, bias=_duplicate_heads(b_kv),
                         out_dtype=BF16, name="kv_proj")
    return h.reshape(batch, seq, d)
```

```python
import functools
import math

import jax
import jax.numpy as jnp
from jax import lax
from jax.experimental import pallas as pl
from jax.experimental.pallas import tpu as pltpu

F32 = jnp.float32
BF16 = jnp.bfloat16

RMS_EPS = 1e-5
NEG_INF = -1e30
LOG2E = math.log2(math.e)

CHUNK = 128
A_GROUPS = 8
HEAD_DIM = 64
KV_GROUP = 8
WINDOW = 128
N_BUCKETS = 32
MAX_DISTANCE = 128

V7X_VMEM_BYTES = 64 * 1024 * 1024
VMEM_LIMIT_BYTES = V7X_VMEM_BYTES - 8 * 1024 * 1024

TM_LINEAR = 1024
TN_LINEAR = 1024
TM_GMLP = 1024
TN_GMLP = 1024
GMLP_ROWS = 512
GMLP_COLS = 512
TM_FFN = 1024
TF_FFN = 512
FFN_DOWN_COLS = 512
NORM_ROWS = 256


def _params():
    return pltpu.CompilerParams(
        dimension_semantics=("arbitrary", "arbitrary"),
        vmem_limit_bytes=VMEM_LIMIT_BYTES,
    )


def _rmsnorm_rows(x_ref, g_ref, o_ref):
    rows = x_ref.shape[0]
    g = g_ref[...]

    def body(i, carry):
        r = pl.ds(pl.multiple_of(i * NORM_ROWS, NORM_ROWS), NORM_ROWS)
        x = x_ref[r, :]
        ms = jnp.mean(x * x, axis=-1, keepdims=True)
        o_ref[r, :] = ((x * lax.rsqrt(ms + RMS_EPS)) * g).astype(o_ref.dtype)
        return carry

    lax.fori_loop(0, rows // NORM_ROWS, body, 0)


def _linear_kernel(*refs, norm, bias, res, scale):
    refs = list(refs)
    x_ref = refs.pop(0)
    g_ref = refs.pop(0) if norm else None
    w_ref = refs.pop(0)
    b_ref = refs.pop(0) if bias else None
    r_ref = refs.pop(0) if res else None
    o_ref = refs.pop(0)
    wbf_ref = refs.pop(0)
    xn_ref = refs.pop(0) if norm else None
    m, j = pl.program_id(0), pl.program_id(1)

    @pl.when(m == 0)
    def _():
        wbf_ref[j] = w_ref[...].astype(BF16)

    if norm:
        @pl.when(j == 0)
        def _():
            _rmsnorm_rows(x_ref, g_ref, xn_ref)
        lhs = xn_ref[...]
    else:
        lhs = x_ref[...]
    acc = jnp.dot(lhs, wbf_ref[j], preferred_element_type=F32)
    if bias:
        acc = acc + b_ref[...]
    if scale != 1.0:
        acc = acc * scale
    if res:
        acc = acc + r_ref[...]
    o_ref[...] = acc.astype(o_ref.dtype)


def _linear(x, w, layer, *, gain=None, bias=None, residual=None, scale=1.0, out_dtype, name):
    t, k = x.shape
    n = w.shape[2]
    tm = min(TM_LINEAR, t)
    tn = min(TN_LINEAR, n)
    nj = n // tn
    norm = gain is not None
    in_specs = [pl.BlockSpec((tm, k), lambda m, j: (m, 0))]
    args = [x]
    if norm:
        in_specs.append(pl.BlockSpec((1, k), lambda m, j: (0, 0)))
        args.append(gain.reshape(1, k))
    in_specs.append(pl.BlockSpec((None, k, tn), lambda m, j: (layer, 0, jnp.where(m == 0, j, nj - 1))))
    args.append(w)
    if bias is not None:
        in_specs.append(pl.BlockSpec((1, tn), lambda m, j: (0, j)))
        args.append(bias.reshape(1, n))
    if residual is not None:
        in_specs.append(pl.BlockSpec((tm, tn), lambda m, j: (m, j)))
        args.append(residual)
    return pl.pallas_call(
        functools.partial(_linear_kernel, norm=norm, bias=bias is not None,
                          res=residual is not None, scale=scale),
        grid=(t // tm, nj),
        in_specs=in_specs,
        out_specs=pl.BlockSpec((tm, tn), lambda m, j: (m, j)),
        out_shape=jax.ShapeDtypeStruct((t, n), out_dtype),
        scratch_shapes=[pltpu.VMEM((nj, k, tn), BF16)] + ([pltpu.VMEM((tm, k), BF16)] if norm else []),
        compiler_params=_params(),
        name=name,
    )(*args)


def _gmlp_in_kernel(h_ref, g_ref, w_ref, nv_ref, ws_ref, bs_ref, o_ref,
                    xn_ref, vbuf_ref, ssq_ref, *, nj):
    j = pl.program_id(1)
    tm = h_ref.shape[0]
    tn = w_ref.shape[1]
    gw = nv_ref.shape[2]
    gpt = tn // gw

    @pl.when(j == 0)
    def _():
        _rmsnorm_rows(h_ref, g_ref, xn_ref)
        ssq_ref[...] = jnp.zeros_like(ssq_ref)

    def gelu_block(cb, r):
        w = w_ref[:, cb * GMLP_COLS:(cb + 1) * GMLP_COLS].astype(BF16)
        x = xn_ref[r * GMLP_ROWS:(r + 1) * GMLP_ROWS, :]
        return jax.nn.gelu(jnp.dot(x, w, preferred_element_type=F32))

    @pl.when(j < nj)
    def _():
        for r in range(tm // GMLP_ROWS):
            rows = slice(r * GMLP_ROWS, (r + 1) * GMLP_ROWS)
            ssq = ssq_ref[rows, :]
            for cb in range(tn // GMLP_COLS):
                v = gelu_block(cb, r)
                vbuf_ref[j, rows, cb * GMLP_COLS:(cb + 1) * GMLP_COLS] = v
                ssq = ssq + jnp.sum(v * v, axis=-1, keepdims=True)
            ssq_ref[rows, :] = ssq

    @pl.when(j >= nj)
    def _():
        ju = j - nj
        row = lax.broadcasted_iota(jnp.int32, (CHUNK, CHUNK), 0)
        col = lax.broadcasted_iota(jnp.int32, (CHUNK, CHUNK), 1)
        for r in range(tm // GMLP_ROWS):
            rows = slice(r * GMLP_ROWS, (r + 1) * GMLP_ROWS)
            rinv = lax.rsqrt(ssq_ref[rows, :] / (nj * tn) + RMS_EPS)
            for cb in range(tn // GMLP_COLS):
                u = gelu_block(cb, r)
                for gl in range(GMLP_COLS // gw):
                    g = ju * gpt + cb * (GMLP_COLS // gw) + gl
                    cols = slice(cb * GMLP_COLS + gl * gw, cb * GMLP_COLS + (gl + 1) * gw)
                    vn = ((vbuf_ref[ju, rows, cols] * rinv) * nv_ref[g]).astype(BF16)
                    wt = jnp.where(col <= row, ws_ref[g], 0.0).astype(BF16)
                    b = bs_ref[g]
                    for c in range(GMLP_ROWS // CHUNK):
                        crows = slice(c * CHUNK, (c + 1) * CHUNK)
                        s = jnp.dot(wt, vn[crows], preferred_element_type=F32) + b
                        o_ref[r * GMLP_ROWS + c * CHUNK:r * GMLP_ROWS + (c + 1) * CHUNK, cols] = (
                            u[crows, gl * gw:(gl + 1) * gw] * s).astype(o_ref.dtype)


def _gmlp_in(h, gain, w_in, layer, norm_v, w_s, b_s):
    t, d = h.shape
    half = w_in.shape[2] // 2
    groups = w_s.shape[0]
    gw = half // groups
    tm, tn = TM_GMLP, TN_GMLP
    nj = half // tn
    return pl.pallas_call(
        functools.partial(_gmlp_in_kernel, nj=nj),
        grid=(t // tm, 2 * nj),
        in_specs=[
            pl.BlockSpec((tm, d), lambda m, j: (m, 0)),
            pl.BlockSpec((1, d), lambda m, j: (0, 0)),
            pl.BlockSpec((None, d, tn), lambda m, j: (layer, 0, (j + nj) % (2 * nj))),
            pl.BlockSpec((groups, 1, gw), lambda m, j: (0, 0, 0)),
            pl.BlockSpec((groups, CHUNK, CHUNK), lambda m, j: (0, 0, 0)),
            pl.BlockSpec((groups, CHUNK, 1), lambda m, j: (0, 0, 0)),
        ],
        out_specs=pl.BlockSpec((tm, tn), lambda m, j: (m, jnp.maximum(j - nj, 0))),
        out_shape=jax.ShapeDtypeStruct((t, half), BF16),
        scratch_shapes=[
            pltpu.VMEM((tm, d), BF16),
            pltpu.VMEM((nj, tm, tn), F32),
            pltpu.VMEM((tm, 1), F32),
        ],
        compiler_params=_params(),
        name="gmlp_in",
    )(h, gain.reshape(1, d), w_in, norm_v.reshape(groups, 1, gw), w_s, b_s.reshape(groups, CHUNK, 1))


def _ffn_kernel(*refs, final):
    refs = list(refs)
    h_hbm, g_ref, wg_ref, wu_ref, wd_ref = refs[:5]
    fg_ref = refs[5] if final else None
    o_hbm, acc_ref, xn_ref, in_sem, out_sem = refs[-5:]
    m, f = pl.program_id(0), pl.program_id(1)
    n_m, n_f = pl.num_programs(0), pl.num_programs(1)
    tm, d = acc_ref.shape[1:]

    def rows_of(tile):
        return pl.ds(pl.multiple_of(tile * tm, tm), tm)

    def h_copy(tile, slot):
        return pltpu.make_async_copy(h_hbm.at[rows_of(tile), :], acc_ref.at[slot], in_sem.at[slot])

    def o_copy(tile, slot):
        return pltpu.make_async_copy(acc_ref.at[slot], o_hbm.at[rows_of(tile), :], out_sem.at[slot])

    def row_tile(slot):
        acc = acc_ref.at[slot]
        other = 1 - slot

        @pl.when(f == 0)
        def _():
            @pl.when(m == 0)
            def _():
                h_copy(m, slot).start()
            h_copy(m, slot).wait()
            _rmsnorm_rows(acc, g_ref, xn_ref)

        @pl.when(f == 1)
        def _():
            @pl.when(m >= 1)
            def _():
                o_copy(m - 1, other).wait()

            @pl.when(m + 1 < n_m)
            def _():
                h_copy(m + 1, other).start()

        xn = xn_ref[...]
        gate = jnp.dot(xn, wg_ref[...].astype(BF16), preferred_element_type=F32)
        up = jnp.dot(xn, wu_ref[...].astype(BF16), preferred_element_type=F32)
        a = (jax.nn.silu(gate) * up).astype(BF16)
        for n in range(d // FFN_DOWN_COLS):
            cols = slice(n * FFN_DOWN_COLS, (n + 1) * FFN_DOWN_COLS)
            acc[:, cols] += jnp.dot(a, wd_ref[:, cols].astype(BF16), preferred_element_type=F32)

        @pl.when(f == n_f - 1)
        def _():
            if final:
                _rmsnorm_rows(acc, fg_ref, acc)
            o_copy(m, slot).start()

            @pl.when(m == n_m - 1)
            def _():
                o_copy(m, slot).wait()

    for slot in range(2):
        pl.when(m % 2 == slot)(functools.partial(row_tile, slot))


def _ffn(h, gain, wg, wu, wd, layer, final_gain=None):
    t, d = h.shape
    ff = wg.shape[2]
    tm, tf = TM_FFN, TF_FFN
    assert ff // tf >= 2, "the prefetch of the next row tile is issued on the second hidden tile"
    final = final_gain is not None
    in_specs = [
        pl.BlockSpec(memory_space=pl.ANY),
        pl.BlockSpec((1, d), lambda m, f: (0, 0)),
        pl.BlockSpec((None, d, tf), lambda m, f: (layer, 0, f)),
        pl.BlockSpec((None, d, tf), lambda m, f: (layer, 0, f)),
        pl.BlockSpec((None, tf, d), lambda m, f: (layer, f, 0)),
    ]
    args = [h, gain.reshape(1, d), wg, wu, wd]
    if final:
        in_specs.append(pl.BlockSpec((1, d), lambda m, f: (0, 0)))
        args.append(final_gain.reshape(1, d))
    return pl.pallas_call(
        functools.partial(_ffn_kernel, final=final),
        grid=(t // tm, ff // tf),
        in_specs=in_specs,
        out_specs=pl.BlockSpec(memory_space=pl.ANY),
        out_shape=jax.ShapeDtypeStruct((t, d), F32),
        scratch_shapes=[
            pltpu.VMEM((2, tm, d), F32),
            pltpu.VMEM((tm, d), BF16),
            pltpu.SemaphoreType.DMA((2,)),
            pltpu.SemaphoreType.DMA((2,)),
        ],
        compiler_params=_params(),
        name="ffn",
    )(*args)


LANES = 128


def _attn_kernel(sink_ref, q_ref, kvp_ref, kvc_ref, bias_ref, o_ref, p_ref, *, n_kv):
    w = WINDOW
    pairs = KV_GROUP * HEAD_DIM // LANES
    low2 = lax.broadcasted_iota(jnp.int32, (2 * w, LANES), 1) < HEAD_DIM
    low1 = lax.broadcasted_iota(jnp.int32, (w, LANES), 1) < HEAD_DIM
    zero = jnp.zeros((2 * w, LANES), BF16)

    def block_diag(prev, cur):
        x = jnp.concatenate([prev, cur], axis=0)
        return jnp.concatenate([jnp.where(low2, x, zero), jnp.where(low2, zero, x)], axis=0)

    for hk in range(n_kv):
        kcols = slice(hk * LANES, (hk + 1) * LANES)
        vcols = slice((n_kv + hk) * LANES, (n_kv + hk + 1) * LANES)
        kk = block_diag(kvp_ref[0, :, kcols], kvc_ref[0, :, kcols])
        vv = block_diag(kvp_ref[0, :, vcols], kvc_ref[0, :, vcols])
        qg = jnp.concatenate(
            [q_ref[0, :, (hk * pairs + p) * LANES:(hk * pairs + p + 1) * LANES] for p in range(pairs)], axis=0)
        s = lax.dot_general(qg, kk, (((1,), (1,)), ((), ())), preferred_element_type=F32)
        rinv = []
        for p in range(pairs):
            rows = slice(p * w, (p + 1) * w)
            for e in range(2):
                cols = slice(e * 2 * w, (e + 1) * 2 * w)
                sink = sink_ref[hk * KV_GROUP + 2 * p + e]
                sc = s[rows, cols] + bias_ref[hk, rows, cols]
                m = jnp.maximum(jnp.max(sc, axis=-1, keepdims=True), sink)
                pe = jnp.exp2(sc - m)
                denom = jnp.sum(pe, axis=-1, keepdims=True) + jnp.exp2(sink - m)
                p_ref[hk, rows, cols] = pe.astype(BF16)
                rinv.append(1.0 / denom)
        o = jnp.dot(p_ref[hk], vv, preferred_element_type=F32)
        for p in range(pairs):
            scale = jnp.where(low1, rinv[2 * p], rinv[2 * p + 1])
            ocols = slice((hk * pairs + p) * LANES, (hk * pairs + p + 1) * LANES)
            o_ref[0, :, ocols] = (o[p * w:(p + 1) * w] * scale).astype(o_ref.dtype)


def _attention(q, kv2, sinks, bias, *, batch, seq):
    t, qd = q.shape
    kvw = kv2.shape[1]
    n_kv = kvw // (2 * LANES)
    pairs = KV_GROUP * HEAD_DIM // LANES
    nb = seq // WINDOW
    q3 = q.reshape(batch, seq, qd)
    kv3 = kv2.reshape(batch, seq, kvw)
    out = pl.pallas_call(
        functools.partial(_attn_kernel, n_kv=n_kv),
        grid=(batch, nb),
        in_specs=[
            pl.BlockSpec(memory_space=pltpu.SMEM),
            pl.BlockSpec((1, WINDOW, qd), lambda b, c: (b, c, 0)),
            pl.BlockSpec((1, WINDOW, kvw), lambda b, c: (b, jnp.maximum(c - 1, 0), 0)),
            pl.BlockSpec((1, WINDOW, kvw), lambda b, c: (b, c, 0)),
            pl.BlockSpec((None, n_kv, pairs * WINDOW, 4 * WINDOW), lambda b, c: (jnp.minimum(c, 1), 0, 0, 0)),
        ],
        out_specs=pl.BlockSpec((1, WINDOW, qd), lambda b, c: (b, c, 0)),
        out_shape=jax.ShapeDtypeStruct((batch, seq, qd), BF16),
        scratch_shapes=[pltpu.VMEM((n_kv, pairs * WINDOW, 4 * WINDOW), BF16)],
        compiler_params=_params(),
        name="swa_attention",
    )(sinks, q3, kv3, kv3, bias)
    return out.reshape(t, qd)


def _t5_bucket(dist):
    max_exact = N_BUCKETS // 2
    is_small = dist < max_exact
    d = jnp.maximum(dist, 1).astype(F32)
    large = max_exact + (jnp.log(d / max_exact) / math.log(MAX_DISTANCE / max_exact)
                         * (N_BUCKETS - max_exact)).astype(jnp.int32)
    large = jnp.minimum(large, N_BUCKETS - 1)
    return jnp.where(is_small, dist, large)


def _band_bias(rel_bias, n_kv):
    w = WINDOW
    n_heads = rel_bias.shape[1]
    by_dist = rel_bias[_t5_bucket(jnp.arange(w, dtype=jnp.int32))].astype(F32).T
    neg = jnp.full((n_heads, 1), NEG_INF, F32)
    r = jnp.concatenate([neg, by_dist[:, ::-1], jnp.broadcast_to(neg, (n_heads, w))], axis=1)
    band = jnp.tile(r, (1, w))[:, :w * 2 * w].reshape(n_heads, w, 2 * w)
    pairs = n_heads // n_kv // 2
    band = band.reshape(n_kv, pairs, 2, w, 2 * w).transpose(0, 1, 3, 2, 4)
    first = band.at[..., :w].set(NEG_INF)
    return (jnp.stack([first, band]) * LOG2E).reshape(2, n_kv, pairs * w, 4 * w)


def _duplicate_heads(a):
    lead = a.shape[:-1]
    n = a.shape[-1] // HEAD_DIM
    a = a.reshape(*lead, n, 1, HEAD_DIM)
    return jnp.broadcast_to(a, (*lead, n, LANES // HEAD_DIM, HEAD_DIM)).reshape(*lead, n * LANES)


def kernel(x, mix_norm, ffn_norm, a_w_in, a_norm_v, a_w_s, a_b_s, a_w_out, kv_norm, w_kv, b_kv, b_w_q, b_b_q, b_sinks, b_w_o, b_b_o, rel_bias, ffn_w_gate, ffn_w_up, ffn_w_down, final_norm):
    batch, seq, d = x.shape
    depth = mix_norm.shape[0]
    n_a = a_w_in.shape[0]
    h = x.reshape(batch * seq, d)
    n_kv = w_kv.shape[1] // (2 * HEAD_DIM)
    bias = _band_bias(rel_bias, n_kv)
    q_scale = LOG2E / math.sqrt(HEAD_DIM)
    kv = None
    for layer in range(depth):
        if layer < n_a:
            i = layer
            gated = _gmlp_in(h, mix_norm[layer], a_w_in, i, a_norm_v[i], a_w_s[i], a_b_s[i])
            h = _linear(gated, a_w_out, i, residual=h, out_dtype=F32, name="gmlp_out")
        else:
            i = layer - n_a
            q = _linear(h, b_w_q, i, gain=mix_norm[layer], bias=b_b_q[i], scale=q_scale,
                        out_dtype=BF16, name="q_proj")
            attn = _attention(q, kv, b_sinks[i] * LOG2E, bias, batch=batch, seq=seq)
            h = _linear(attn, b_w_o, i, bias=b_b_o[i], residual=h, out_dtype=F32, name="o_proj")
        h = _ffn(h, ffn_norm[layer], ffn_w_gate, ffn_w_up, ffn_w_down, layer,
                 final_gain=final_norm if layer == depth - 1 else None)
        if layer == n_a - 1:
            kv = _linear(h, _duplicate_heads(w_kv)[None], 0, gain=kv_norm, bias=_duplicate_heads(b_kv),
                         out_dtype=BF16, name="kv_proj")
    return h.reshape(batch, seq, d)
```
